```python
import jax
import jax.numpy as jnp
from jax import lax
import numpy as np

D_MODEL = 4096
BATCH = 2
SEQ = 8192
DEPTH = 2
DEC_BATCH = 8
DEC_SEQ = 32
PAST_LEN = 2048

CHUNK = 64
BAND_CHUNKS = 8
BAND_PAST = BAND_CHUNKS * CHUNK
HEAD_DIM = 128
H_A = 16
H_B = 16
D_AB = (H_A + H_B) * HEAD_DIM
REL_CLIP = 128
N_REL = 2 * REL_CLIP + 1
H_C = 32
Q_LORA = 1024
KV_LORA = 512
NOPE_DIM = 128
ROPE_DIM = 64
QK_DIM = NOPE_DIM + ROPE_DIM
V_DIM = 128
D_FF = 4 * D_MODEL
Q_BLOCK = 128
ROPE_THETA = 10000.0
EPS = 1e-6

kernel_name = 'hybrid_chunk_stream_encoder_step'


def rms_norm(x, g):
    x32 = x.astype(jnp.float32)
    y = x32 * lax.rsqrt(jnp.mean(x32 * x32, axis=-1, keepdims=True) + EPS)
    return y.astype(x.dtype) * g


def ada_norm(x, c, g, w, b):
    shift, scale, gate = jnp.split((jax.nn.silu(c) @ w + b)[:, None, :], 3, axis=-1)
    return rms_norm(x, g) * (1.0 + scale) + shift, gate


def rope(x, pos):
    half = ROPE_DIM // 2
    inv = ROPE_THETA ** (-jnp.arange(half, dtype=jnp.float32) / half)
    ang = pos.astype(jnp.float32)[:, None] * inv[None, :]
    ang = ang.reshape(ang.shape[:1] + (1,) * (x.ndim - 3) + (half,))
    cos, sin = jnp.cos(ang), jnp.sin(ang)
    x32 = x.astype(jnp.float32)
    x1, x2 = x32[..., :half], x32[..., half:]
    return jnp.concatenate([x1 * cos - x2 * sin, x2 * cos + x1 * sin], -1).astype(x.dtype)


def softmax_attend(q, k, v, bias, valid, scale):
    s = jnp.einsum('...qhd,...khd->...hqk', q, k).astype(jnp.float32) * scale + bias
    s = jnp.where(valid, s, -jnp.inf)
    p = jax.nn.softmax(s, axis=-1).astype(v.dtype)
    return jnp.einsum('...hqk,...khd->...qhd', p, v)


def rel_bias_table(rel_bias, dist):
    return rel_bias[:, jnp.clip(dist, -REL_CLIP, REL_CLIP) + REL_CLIP].astype(jnp.float32)


def ab_heads(h, w_in, g_q, g_k):
    b, s, _ = h.shape
    qkv = (h @ w_in).reshape(b, s, 3, H_A + H_B, HEAD_DIM)
    q, k, v = qkv[:, :, 0], qkv[:, :, 1], qkv[:, :, 2]
    qa = rms_norm(q[:, :, :H_A], g_q)
    ka = rms_norm(k[:, :, :H_A], g_k)
    return qa, ka, v[:, :, :H_A], q[:, :, H_A:], k[:, :, H_A:], v[:, :, H_A:]


def band_attn_prompt(q, k, v, rel_bias, n_keep):
    b, s, h, d = q.shape
    nc = s // CHUNK
    zpad = jnp.zeros((b, BAND_PAST, h, d), k.dtype)
    kp = jnp.concatenate([zpad, k], 1)
    vp = jnp.concatenate([zpad, v], 1)

    def band(t):
        tc = t.reshape(b, nc + BAND_CHUNKS, CHUNK, h, d)
        return jnp.concatenate([tc[:, i:i + nc] for i in range(BAND_CHUNKS + 1)], axis=2)

    kb, vb = band(kp), band(vp)
    qc = q.reshape(b, nc, CHUNK, h, d)
    j = jnp.arange((BAND_CHUNKS + 1) * CHUNK)
    i = jnp.arange(CHUNK)
    bias = rel_bias_table(rel_bias, i[:, None] - j[None, :] + BAND_PAST)
    kpos = jnp.arange(nc)[:, None] * CHUNK + j[None, :] - BAND_PAST
    valid = (kpos >= 0)[:, None, None, :]
    o = softmax_attend(qc, kb, vb, bias, valid, HEAD_DIM ** -0.5).reshape(b, s, h, d)
    return o, kp[:, kp.shape[1] - n_keep:], vp[:, vp.shape[1] - n_keep:]


def band_attn_sample(q, k, v, ck, cv, rel_bias):
    t = q.shape[1]
    r = ck.shape[1]
    kk = jnp.concatenate([ck, k], 1)
    vv = jnp.concatenate([cv, v], 1)
    qpos = PAST_LEN + jnp.arange(t)
    kpos = PAST_LEN - r + jnp.arange(r + t)
    bias = rel_bias_table(rel_bias, qpos[:, None] - kpos[None, :])
    return softmax_attend(q, kk, vv, bias, True, HEAD_DIM ** -0.5)


def stick_breaking(q, k, v, qpos, kpos):
    z = jnp.einsum('bqhd,bkhd->bhqk', q, k).astype(jnp.float32) * (HEAD_DIM ** -0.5)
    causal = kpos[None, :] < qpos[:, None]
    log_keep = jnp.where(causal, jax.nn.log_sigmoid(-z), 0.0)
    rev = lax.cumsum(log_keep, axis=log_keep.ndim - 1, reverse=True)
    after = jnp.concatenate([rev[..., 1:], jnp.zeros_like(rev[..., :1])], -1)
    w = jnp.where(causal, jnp.exp(jax.nn.log_sigmoid(z) + after), 0.0)
    return jnp.einsum('bhqk,bkhd->bqhd', w.astype(v.dtype), v)


def chunk_causal_attend(q, k, v, qpos, kpos):
    valid = (kpos // CHUNK)[None, :] <= (qpos // CHUNK)[:, None]
    return softmax_attend(q, k, v, 0.0, valid, QK_DIM ** -0.5)


def query_blocks(attend, q, k, v):
    b, s, h, d = q.shape
    nb = s // Q_BLOCK
    qb = jnp.moveaxis(q.reshape(b, nb, Q_BLOCK, h, d), 1, 0)
    kpos = jnp.arange(s)
    o = lax.map(lambda a: attend(a[0], k, v, a[1] + jnp.arange(Q_BLOCK), kpos),
                (qb, jnp.arange(nb) * Q_BLOCK))
    return jnp.moveaxis(o, 0, 1).reshape(b, s, h, v.shape[-1])


def mla_project(h, pos, wq_a, g_q_lat, wq_b, wkv_a, g_kv_lat, g_q):
    b, s, _ = h.shape
    q = (rms_norm(h @ wq_a, g_q_lat) @ wq_b).reshape(b, s, H_C, QK_DIM)
    q = rms_norm(jnp.concatenate([q[..., :NOPE_DIM], rope(q[..., NOPE_DIM:], pos)], -1), g_q)
    kv = h @ wkv_a
    c_kv = rms_norm(kv[..., :KV_LORA], g_kv_lat)
    k_pe = rope(kv[..., KV_LORA:], pos)
    return q, c_kv, k_pe


def mla_keys(c_kv, k_pe, wkv_b, g_k):
    b, s, _ = c_kv.shape
    kv = (c_kv @ wkv_b).reshape(b, s, H_C, NOPE_DIM + V_DIM)
    k = jnp.concatenate([kv[..., :NOPE_DIM],
                         jnp.broadcast_to(k_pe[:, :, None, :], (b, s, H_C, ROPE_DIM))], -1)
    return rms_norm(k, g_k), kv[..., NOPE_DIM:]


def sq_relu_mlp(h, w1, w2):
    return jnp.square(jax.nn.relu(h @ w1)) @ w2


def setup_inputs(seed: int = 0) -> dict:
    key = jax.random.key(seed)
    ks = iter(jax.random.split(key, 64))
    f32 = jnp.float32
    n_even = (DEPTH + 1) // 2
    n_odd = DEPTH // 2
    a_rows = min(BAND_PAST, PAST_LEN)

    def nrm(shape, scale=1.0):
        return jax.random.normal(next(ks), shape, f32) * scale

    def gain(shape):
        return 1.0 + 0.05 * nrm(shape)

    return {
        'x_prompt': nrm((BATCH, SEQ, D_MODEL)),
        'x_sample': nrm((DEC_BATCH, DEC_SEQ, D_MODEL)),
        'cache_a_k': nrm((n_even, DEC_BATCH, a_rows, H_A, HEAD_DIM)),
        'cache_a_v': nrm((n_even, DEC_BATCH, a_rows, H_A, HEAD_DIM)),
        'cache_b_k': nrm((n_even, DEC_BATCH, PAST_LEN, H_B, HEAD_DIM)),
        'cache_b_v': nrm((n_even, DEC_BATCH, PAST_LEN, H_B, HEAD_DIM)),
        'cache_c_kv': nrm((n_odd, DEC_BATCH, PAST_LEN, KV_LORA)),
        'cache_c_pe': nrm((n_odd, DEC_BATCH, PAST_LEN, ROPE_DIM)),
        'c_prompt': nrm((BATCH, D_MODEL)),
        'c_sample': nrm((DEC_BATCH, D_MODEL)),
        'norm_g': gain((DEPTH, 2, D_MODEL)),
        'ada_w': nrm((DEPTH, 2, D_MODEL, 3 * D_MODEL), 0.5 * D_MODEL ** -0.5),
        'ada_b': nrm((DEPTH, 2, 3 * D_MODEL), 0.02),
        'w_in_ab': nrm((n_even, D_MODEL, 3 * D_AB), D_MODEL ** -0.5),
        'g_q_a': gain((n_even, HEAD_DIM)),
        'g_k_a': gain((n_even, HEAD_DIM)),
        'rel_bias_a': nrm((n_even, H_A, N_REL), 0.1),
        'w_out_ab': nrm((n_even, D_AB, D_MODEL), D_AB ** -0.5),
        'wq_a_c': nrm((n_odd, D_MODEL, Q_LORA), D_MODEL ** -0.5),
        'g_q_lat_c': gain((n_odd, Q_LORA)),
        'wq_b_c': nrm((n_odd, Q_LORA, H_C * QK_DIM), Q_LORA ** -0.5),
        'wkv_a_c': nrm((n_odd, D_MODEL, KV_LORA + ROPE_DIM), D_MODEL ** -0.5),
        'g_kv_lat_c': gain((n_odd, KV_LORA)),
        'wkv_b_c': nrm((n_odd, KV_LORA, H_C * (NOPE_DIM + V_DIM)), KV_LORA ** -0.5),
        'g_q_c': gain((n_odd, QK_DIM)),
        'g_k_c': gain((n_odd, QK_DIM)),
        'w_out_c': nrm((n_odd, H_C * V_DIM, D_MODEL), (H_C * V_DIM) ** -0.5),
        'mlp_w1': nrm((DEPTH, D_MODEL, D_FF), D_MODEL ** -0.5),
        'mlp_w2': nrm((DEPTH, D_FF, D_MODEL), D_FF ** -0.5),
    }


def reference(x_prompt, x_sample, cache_a_k, cache_a_v, cache_b_k, cache_b_v, cache_c_kv, cache_c_pe,
              c_prompt, c_sample, norm_g, ada_w, ada_b, w_in_ab, g_q_a, g_k_a, rel_bias_a, w_out_ab,
              wq_a_c, g_q_lat_c, wq_b_c, wkv_a_c, g_kv_lat_c, wkv_b_c, g_q_c, g_k_c, w_out_c,
              mlp_w1, mlp_w2):
    bp, s, _ = x_prompt.shape
    bs, t, _ = x_sample.shape
    n_keep = cache_a_k.shape[2]
    pos_p = jnp.arange(s)
    pos_s = PAST_LEN + jnp.arange(t)
    kpos_s = jnp.arange(PAST_LEN + t)
    xp, xs = x_prompt, x_sample
    ak_p, av_p, ak_s, av_s = [], [], [], []
    bk_p, bv_p, bk_s, bv_s = [], [], [], []
    ckv_p, cpe_p, ckv_s, cpe_s = [], [], [], []
    for layer in range(DEPTH):
        hp, gate_p = ada_norm(xp, c_prompt, norm_g[layer, 0], ada_w[layer, 0], ada_b[layer, 0])
        hs, gate_s = ada_norm(xs, c_sample, norm_g[layer, 0], ada_w[layer, 0], ada_b[layer, 0])
        if layer % 2 == 0:
            e = layer // 2
            qa, ka, va, qb, kb, vb = ab_heads(hp, w_in_ab[e], g_q_a[e], g_k_a[e])
            oa, ka_keep, va_keep = band_attn_prompt(qa, ka, va, rel_bias_a[e], n_keep)
            ob = query_blocks(stick_breaking, qb, kb, vb)
            mix_p = jnp.concatenate([oa, ob], 2).reshape(bp, s, D_AB) @ w_out_ab[e]
            ak_p.append(ka_keep)
            av_p.append(va_keep)
            bk_p.append(kb)
            bv_p.append(vb)
            qa, ka, va, qb, kb, vb = ab_heads(hs, w_in_ab[e], g_q_a[e], g_k_a[e])
            oa = band_attn_sample(qa, ka, va, cache_a_k[e], cache_a_v[e], rel_bias_a[e])
            ob = stick_breaking(qb, jnp.concatenate([cache_b_k[e], kb], 1),
                                jnp.concatenate([cache_b_v[e], vb], 1), pos_s, kpos_s)
            mix_s = jnp.concatenate([oa, ob], 2).reshape(bs, t, D_AB) @ w_out_ab[e]
            ak_s.append(ka)
            av_s.append(va)
            bk_s.append(kb)
            bv_s.append(vb)
        else:
            m = layer // 2
            q, ckv, kpe = mla_project(hp, pos_p, wq_a_c[m], g_q_lat_c[m], wq_b_c[m], wkv_a_c[m],
                                      g_kv_lat_c[m], g_q_c[m])
            k, v = mla_keys(ckv, kpe, wkv_b_c[m], g_k_c[m])
            o = query_blocks(chunk_causal_attend, q, k, v)
            mix_p = o.reshape(bp, s, H_C * V_DIM) @ w_out_c[m]
            ckv_p.append(ckv)
            cpe_p.append(kpe)
            q, ckv, kpe = mla_project(hs, pos_s, wq_a_c[m], g_q_lat_c[m], wq_b_c[m], wkv_a_c[m],
                                      g_kv_lat_c[m], g_q_c[m])
            k, v = mla_keys(jnp.concatenate([cache_c_kv[m], ckv], 1),
                            jnp.concatenate([cache_c_pe[m], kpe], 1), wkv_b_c[m], g_k_c[m])
            o = chunk_causal_attend(q, k, v, pos_s, kpos_s)
            mix_s = o.reshape(bs, t, H_C * V_DIM) @ w_out_c[m]
            ckv_s.append(ckv)
            cpe_s.append(kpe)
        xp = xp + gate_p * mix_p
        xs = xs + gate_s * mix_s
        hp, gate_p = ada_norm(xp, c_prompt, norm_g[layer, 1], ada_w[layer, 1], ada_b[layer, 1])
        hs, gate_s = ada_norm(xs, c_sample, norm_g[layer, 1], ada_w[layer, 1], ada_b[layer, 1])
        xp = xp + gate_p * sq_relu_mlp(hp, mlp_w1[layer], mlp_w2[layer])
        xs = xs + gate_s * sq_relu_mlp(hs, mlp_w1[layer], mlp_w2[layer])
    return (xp, xs,
            jnp.stack(ak_p), jnp.stack(av_p), jnp.stack(ak_s), jnp.stack(av_s),
            jnp.stack(bk_p), jnp.stack(bv_p), jnp.stack(bk_s), jnp.stack(bv_s),
            jnp.stack(ckv_p), jnp.stack(cpe_p), jnp.stack(ckv_s), jnp.stack(cpe_s))
```

```python
import functools
import math

import jax
import jax.numpy as jnp
from jax import lax
from jax.experimental import pallas as pl
from jax.experimental.pallas import tpu as pltpu

CHUNK = 64
BAND_CHUNKS = 8
BAND_PAST = BAND_CHUNKS * CHUNK
HEAD_DIM = 128
H_A = 16
H_B = 16
REL_CLIP = 128
N_REL = 2 * REL_CLIP + 1
H_C = 32
NOPE_DIM = 128
ROPE_DIM = 64
QK_DIM = NOPE_DIM + ROPE_DIM
V_DIM = 128
ROPE_THETA = 10000.0
EPS = 1e-6

LANES = 128
MXU_DIM = 256
VMEM_LIMIT_BYTES = 52 * 1024 * 1024

SB_EXIT = 105.0

BAND_Q = 2 * CHUNK
BAND_W = BAND_PAST + BAND_Q
PADDED_HEAD = 2 * LANES

F32 = jnp.float32
BF16 = jnp.bfloat16
_NT = (((1,), (1,)), ((), ()))


def _chunk_of(pos):
    return jnp.right_shift(pos, int(math.log2(CHUNK)))


def _params(*sem):
    return pltpu.CompilerParams(dimension_semantics=sem, vmem_limit_bytes=VMEM_LIMIT_BYTES)


def _row_tile(m, cap=1024):
    t = min(m, cap)
    while m % t:
        t //= 2
    return t


def _ada_kernel(c_ref, w_ref, b_ref, o_ref, *, d_tiles):
    j = pl.program_id(1)
    c = c_ref[...]
    a = (c * jax.nn.sigmoid(c)).astype(BF16)
    acc = jnp.dot(a, w_ref[...].astype(BF16), preferred_element_type=F32) + b_ref[...]
    is_scale = jnp.logical_and(j >= d_tiles, j < 2 * d_tiles)
    o_ref[...] = acc + jnp.where(is_scale, 1.0, 0.0).astype(F32)


def _ada_params(c_rows, ada_w, ada_b):
    n_l, d, d3 = ada_w.shape
    r = c_rows.shape[0]
    tn = min(512, d)
    return pl.pallas_call(
        functools.partial(_ada_kernel, d_tiles=d // tn),
        out_shape=jax.ShapeDtypeStruct((n_l, r, d3), F32),
        grid=(n_l, d3 // tn),
        in_specs=[pl.BlockSpec((r, d), lambda l, j: (0, 0)),
                  pl.BlockSpec((None, d, tn), lambda l, j: (l, 0, j)),
                  pl.BlockSpec((None, 1, tn), lambda l, j: (l, 0, j))],
        out_specs=pl.BlockSpec((None, r, tn), lambda l, j: (l, 0, j)),
        compiler_params=_params("arbitrary", "arbitrary"),
        name="ada_params",
    )(c_rows, ada_w, ada_b)


def _adanorm_kernel(x_ref, g_ref, sh_ref, sc_ref, o_ref):
    x = x_ref[...]
    ms = jnp.mean(x * x, axis=-1, keepdims=True)
    y = x * lax.rsqrt(ms + EPS) * g_ref[...]
    o_ref[...] = (y * sc_ref[...] + sh_ref[...]).astype(o_ref.dtype)


def _adanorm(x, g, mod4, ls, row0):
    b, s, d = x.shape
    tm = _row_tile(s, 512)
    nt = s // tm
    return pl.pallas_call(
        _adanorm_kernel,
        out_shape=jax.ShapeDtypeStruct((b * s, d), BF16),
        grid=(b, nt),
        in_specs=[pl.BlockSpec((None, tm, d), lambda bi, i: (bi, i, 0)),
                  pl.BlockSpec((None, 1, d), lambda bi, i: (ls, 0, 0)),
                  pl.BlockSpec((None, None, 1, d), lambda bi, i: (ls, row0 + bi, 0, 0)),
                  pl.BlockSpec((None, None, 1, d), lambda bi, i: (ls, row0 + bi, 0, 1))],
        out_specs=pl.BlockSpec((tm, d), lambda bi, i: (bi * nt + i, 0)),
        compiler_params=_params("arbitrary", "arbitrary"),
        name="adanorm",
    )(x, g, mod4, mod4)


def _mm_kernel(*refs, n_extra, epilogue):
    x_ref, w_ref = refs[:2]
    extra = refs[2:2 + n_extra]
    outs = refs[2 + n_extra:]
    acc = jnp.dot(x_ref[...], w_ref[...], preferred_element_type=F32)
    epilogue(acc, extra, outs)


def _matmul(x, w, col0, ncols, tn, epilogue, extras, outs, name):
    m, k = x.shape
    tm = _row_tile(m)
    assert ncols % tn == 0 and col0 % tn == 0
    j0 = col0 // tn
    in_specs = [pl.BlockSpec((tm, k), lambda i, j: (i, 0)),
                pl.BlockSpec((k, tn), lambda i, j: (0, j0 + j))]
    in_specs += [pl.BlockSpec(bs, im) for _, bs, im in extras]
    return pl.pallas_call(
        functools.partial(_mm_kernel, n_extra=len(extras), epilogue=epilogue),
        out_shape=[jax.ShapeDtypeStruct(sh, dt) for sh, dt, _, _ in outs],
        grid=(m // tm, ncols // tn),
        in_specs=in_specs,
        out_specs=[pl.BlockSpec(bs, im) for _, _, bs, im in outs],
        compiler_params=_params("arbitrary", "arbitrary"),
        name=name,
    )(x, w, *[a for a, _, _ in extras])


def _epi_plain(acc, extra, outs, *, scale=1.0):
    for o in outs:
        o[...] = (acc * scale if scale != 1.0 else acc).astype(o.dtype)


def _epi_relu2(acc, extra, outs):
    outs[0][...] = jnp.square(jnp.maximum(acc, 0.0)).astype(outs[0].dtype)


def _epi_headnorm(acc, extra, outs, *, scale=1.0):
    g = extra[0][...]
    for h in range(acc.shape[1] // HEAD_DIM):
        sl = slice(h * HEAD_DIM, (h + 1) * HEAD_DIM)
        a = acc[:, sl]
        y = a * lax.rsqrt(jnp.mean(a * a, axis=-1, keepdims=True) + EPS) * g
        if scale != 1.0:
            y = y * scale
        for o in outs:
            o[:, sl] = y.astype(o.dtype)


def _epi_rownorm(acc, extra, outs):
    g = extra[0][...]
    y = acc * lax.rsqrt(jnp.mean(acc * acc, axis=-1, keepdims=True) + EPS) * g
    for o in outs:
        o[...] = y.astype(o.dtype)


def _rope_pair(v, cos, sin):
    return v * cos + pltpu.roll(v, ROPE_DIM, 1) * sin


def _epi_rope(acc, extra, outs):
    outs[0][...] = _rope_pair(acc, extra[0][...], extra[1][...])


def _epi_q_latent(acc, extra, outs, *, scale):
    cos, sin = extra[0][...], extra[1][...]
    gn, gr = extra[2][...], extra[3][...]
    for h in range(acc.shape[1] // PADDED_HEAD):
        c0 = h * PADDED_HEAD
        n = acc[:, c0:c0 + NOPE_DIM]
        r = _rope_pair(acc[:, c0 + NOPE_DIM:c0 + PADDED_HEAD], cos, sin)
        ss = jnp.sum(n * n, axis=-1, keepdims=True) + jnp.sum(r * r, axis=-1, keepdims=True)
        inv = lax.rsqrt(ss * (1.0 / QK_DIM) + EPS) * scale
        outs[0][:, c0:c0 + NOPE_DIM] = (n * inv * gn).astype(BF16)
        outs[0][:, c0 + NOPE_DIM:c0 + PADDED_HEAD] = (r * inv * gr).astype(BF16)


def _epi_k_latent(acc, extra, outs):
    pe = extra[0][...]
    gn, gr = extra[1][...], extra[2][...]
    sspe = jnp.sum(pe * pe, axis=-1, keepdims=True)
    for h in range(acc.shape[1] // NOPE_DIM):
        n = acc[:, h * NOPE_DIM:(h + 1) * NOPE_DIM]
        ss = jnp.sum(n * n, axis=-1, keepdims=True) + sspe
        inv = lax.rsqrt(ss * (1.0 / QK_DIM) + EPS)
        c0 = h * PADDED_HEAD
        outs[0][:, c0:c0 + NOPE_DIM] = (n * inv * gn).astype(BF16)
        outs[0][:, c0 + NOPE_DIM:c0 + PADDED_HEAD] = (pe * inv * gr).astype(BF16)


def _resid_kernel(a_ref, w_ref, x_ref, g_ref, o_ref, *scratch, nk):
    if nk == 1:
        acc = jnp.dot(a_ref[...], w_ref[...], preferred_element_type=F32)
        o_ref[...] = x_ref[...] + g_ref[...] * acc
        return
    acc_ref, = scratch
    k = pl.program_id(2)

    @pl.when(k == 0)
    def _():
        acc_ref[...] = jnp.zeros_like(acc_ref)

    acc_ref[...] += jnp.dot(a_ref[...], w_ref[...], preferred_element_type=F32)

    @pl.when(k == nk - 1)
    def _():
        o_ref[...] = x_ref[...] + g_ref[...] * acc_ref[...]


def _resid_matmul(a, w, x, gate, gate_spec, name):
    m, k = a.shape
    d = w.shape[1]
    tm = _row_tile(m)
    tn = min(1024, d)
    tk = min(2048, k)
    nk = k // tk
    scratch = [] if nk == 1 else [pltpu.VMEM((tm, tn), F32)]
    return pl.pallas_call(
        functools.partial(_resid_kernel, nk=nk),
        out_shape=jax.ShapeDtypeStruct((m, d), F32),
        grid=(m // tm, d // tn, nk),
        in_specs=[pl.BlockSpec((tm, tk), lambda i, j, kk: (i, kk)),
                  pl.BlockSpec((tk, tn), lambda i, j, kk: (kk, j)),
                  pl.BlockSpec((tm, tn), lambda i, j, kk: (i, j)),
                  gate_spec(tm, tn)],
        out_specs=pl.BlockSpec((tm, tn), lambda i, j, kk: (i, j)),
        scratch_shapes=scratch,
        compiler_params=_params("arbitrary", "arbitrary", "arbitrary"),
        name=name,
    )(a, w, x, gate)


def _bias_kernel(tab_ref, o_ref):
    h = pl.program_id(0)
    row = lax.broadcasted_iota(jnp.int32, (BAND_Q, BAND_W), 0)
    col = lax.broadcasted_iota(jnp.int32, (BAND_Q, BAND_W), 1)
    idx = jnp.clip(BAND_PAST + row - col, -REL_CLIP, REL_CLIP) + REL_CLIP
    rel = BAND_CHUNKS + _chunk_of(row) - _chunk_of(col)
    valid = jnp.logical_and(rel >= 0, rel <= BAND_CHUNKS)

    def body(t, acc):
        return jnp.where(idx == t, tab_ref[h, t], acc)

    acc = lax.fori_loop(0, N_REL, body, jnp.zeros((BAND_Q, BAND_W), F32))
    o_ref[...] = jnp.where(valid, acc, -jnp.inf)


def _band_bias(rel_bias):
    return pl.pallas_call(
        _bias_kernel,
        out_shape=jax.ShapeDtypeStruct((H_A, BAND_Q, BAND_W), F32),
        grid=(H_A,),
        in_specs=[pl.BlockSpec(memory_space=pltpu.SMEM)],
        out_specs=pl.BlockSpec((None, BAND_Q, BAND_W), lambda h: (h, 0, 0)),
        compiler_params=_params("arbitrary"),
        name="band_bias",
    )(rel_bias)


def _softmax_attend(q, k, v, bias):
    s = lax.dot_general(q, k, _NT, preferred_element_type=F32) + bias
    m = jnp.max(s, axis=-1, keepdims=True)
    p = jnp.exp(s - m)
    l = jnp.sum(p, axis=-1, keepdims=True)
    o = jnp.dot(p.astype(BF16), v, preferred_element_type=F32)
    return o / l


def _band_kernel(q_ref, k_ref, v_ref, bias_ref, o_ref, *, nq):
    lead = BAND_PAST // BAND_Q
    for i in range(min(lead, nq)):
        rows = slice(i * BAND_Q, (i + 1) * BAND_Q)
        keys = slice(0, (i + 1) * BAND_Q)
        bias = bias_ref[:, (lead - i) * BAND_Q:]
        o_ref[rows, :] = _softmax_attend(q_ref[rows, :], k_ref[keys, :], v_ref[keys, :],
                                         bias).astype(o_ref.dtype)

    def body(i, c):
        r0 = pl.multiple_of(i * BAND_Q, BAND_Q)
        k0 = pl.multiple_of((i - lead) * BAND_Q, BAND_Q)
        o = _softmax_attend(q_ref[pl.ds(r0, BAND_Q), :], k_ref[pl.ds(k0, BAND_W), :],
                            v_ref[pl.ds(k0, BAND_W), :], bias_ref[...])
        o_ref[pl.ds(r0, BAND_Q), :] = o.astype(o_ref.dtype)
        return c

    if nq > lead:
        lax.fori_loop(lead, nq, body, 0)


def _band_attention(q, k, v, bias, b, s):
    assert s % BAND_Q == 0
    view = lambda t: t.reshape(b, s, H_A * HEAD_DIM)
    spec = pl.BlockSpec((None, s, HEAD_DIM), lambda bi, h: (bi, 0, h))
    out = pl.pallas_call(
        functools.partial(_band_kernel, nq=s // BAND_Q),
        out_shape=jax.ShapeDtypeStruct((b, s, H_A * HEAD_DIM), BF16),
        grid=(b, H_A),
        in_specs=[spec, spec, spec,
                  pl.BlockSpec((None, BAND_Q, BAND_W), lambda bi, h: (h, 0, 0))],
        out_specs=spec,
        compiler_params=_params("arbitrary", "arbitrary"),
        name="band_attn",
    )(view(q), view(k), view(v), bias)
    return out.reshape(b * s, H_A * HEAD_DIM)


def _band_sample_kernel(q_ref, k_ref, v_ref, bias_ref, o_ref, *, n_valid):
    t = q_ref.shape[0]
    col = lax.broadcasted_iota(jnp.int32, (t, BAND_W), 1)
    bias = jnp.where(col < n_valid, bias_ref[0:t, :], -jnp.inf)
    o_ref[...] = _softmax_attend(q_ref[...], k_ref[...], v_ref[...], bias).astype(o_ref.dtype)


def _band_attention_sample(q, kk, vv, bias, bs, t, n_valid):
    kv_spec = pl.BlockSpec((None, BAND_W, HEAD_DIM), lambda bi, h: (bi, 0, h))
    q_spec = pl.BlockSpec((t, HEAD_DIM), lambda bi, h: (bi, h))
    return pl.pallas_call(
        functools.partial(_band_sample_kernel, n_valid=n_valid),
        out_shape=jax.ShapeDtypeStruct((bs * t, H_A * HEAD_DIM), BF16),
        grid=(bs, H_A),
        in_specs=[q_spec, kv_spec, kv_spec,
                  pl.BlockSpec((None, BAND_Q, BAND_W), lambda bi, h: (h, 0, 0))],
        out_specs=q_spec,
        compiler_params=_params("arbitrary", "arbitrary"),
        name="band_attn_sample",
    )(q, kk, vv, bias)


def _sb_kernel(q_ref, k_ref, v_ref, o_ref, acc_ref, run_ref, *, nq, tq, tk, qoff):
    above = (lax.broadcasted_iota(jnp.int32, (tk, tk), 0)
             > lax.broadcasted_iota(jnp.int32, (tk, tk), 1)).astype(BF16)

    def block(q, j, run, causal):
        k0 = pl.multiple_of(j * tk, tk)
        z = lax.dot_general(q, k_ref[pl.ds(k0, tk), :], _NT, preferred_element_type=F32)
        log_sig = jnp.minimum(z, 0.0) - jnp.log1p(jnp.exp(-jnp.abs(z)))
        log_keep = log_sig - z
        if causal is not None:
            log_keep = jnp.where(causal, log_keep, 0.0)
        hi = log_keep.astype(BF16)
        lo = (log_keep - hi.astype(F32)).astype(BF16)
        after = (jnp.dot(hi, above, preferred_element_type=F32)
                 + jnp.dot(lo, above, preferred_element_type=F32)) + run
        w = jnp.exp(log_sig + after)
        if causal is not None:
            w = jnp.where(causal, w, 0.0)
        o = jnp.dot(w.astype(BF16), v_ref[pl.ds(k0, tk), :], preferred_element_type=F32)
        return o, run + jnp.sum(log_keep, axis=-1, keepdims=True)

    def q_tile(qi, c):
        r0 = pl.multiple_of(qi * tq, tq)
        q = q_ref[pl.ds(r0, tq), :]
        jd = (qoff + r0) // tk
        qpos = qoff + r0 + lax.broadcasted_iota(jnp.int32, (tq, tk), 0)
        kpos = jd * tk + lax.broadcasted_iota(jnp.int32, (tq, tk), 1)
        o, run = block(q, jd, jnp.zeros((tq, 1), F32), kpos < qpos)
        acc_ref[...] = o
        run_ref[...] = run

        def cond(st):
            j, run_max = st
            return jnp.logical_and(j >= 0, run_max > -SB_EXIT)

        def body(st):
            j, _ = st
            o, run = block(q, j, run_ref[...], None)
            acc_ref[...] += o
            run_ref[...] = run
            return j - 1, jnp.max(run)

        lax.while_loop(cond, body, (jd - 1, jnp.max(run)))
        o_ref[pl.ds(r0, tq), :] = acc_ref[...].astype(o_ref.dtype)
        return c

    lax.fori_loop(0, nq, q_tile, 0)


def _stick_breaking(q, k, v, b, tq_total, tk_total, tq, tk, qoff):
    assert tq_total % tq == 0 and tk_total % tk == 0 and tq <= tk and tk % tq == 0 and qoff % tk == 0
    assert qoff + tq_total <= tk_total
    w = H_B * HEAD_DIM
    q_spec = pl.BlockSpec((None, tq_total, HEAD_DIM), lambda bi, h: (bi, 0, h))
    kv_spec = pl.BlockSpec((None, tk_total, HEAD_DIM), lambda bi, h: (bi, 0, h))
    return pl.pallas_call(
        functools.partial(_sb_kernel, nq=tq_total // tq, tq=tq, tk=tk, qoff=qoff),
        out_shape=jax.ShapeDtypeStruct((b, tq_total, w), BF16),
        grid=(b, H_B),
        in_specs=[q_spec, kv_spec, kv_spec],
        out_specs=q_spec,
        scratch_shapes=[pltpu.VMEM((tq, HEAD_DIM), F32), pltpu.VMEM((tq, 1), F32)],
        compiler_params=_params("arbitrary", "arbitrary"),
        name="stick_breaking",
    )(q, k, v)


def _online_softmax_step(q, k, v, carry, valid):
    m, l, acc = carry
    s = lax.dot_general(q, k, _NT, preferred_element_type=F32)
    if valid is not None:
        s = jnp.where(valid, s, -jnp.inf)
    m_new = jnp.maximum(m, jnp.max(s, axis=-1, keepdims=True))
    alpha = jnp.exp(m - m_new)
    p = jnp.exp(s - m_new)
    l = alpha * l + jnp.sum(p, axis=-1, keepdims=True)
    acc = alpha * acc + jnp.dot(p.astype(BF16), v, preferred_element_type=F32)
    return m_new, l, acc


def _mla_kernel(q_ref, k_ref, v_ref, o_ref, *, nq, t):
    row = lax.broadcasted_iota(jnp.int32, (t, t), 0)
    col = lax.broadcasted_iota(jnp.int32, (t, t), 1)
    diag_valid = _chunk_of(col) <= _chunk_of(row)

    def q_tile(qi, c):
        r0 = pl.multiple_of(qi * t, t)
        q = q_ref[pl.ds(r0, t), :]

        def kv(j, carry):
            k0 = pl.multiple_of(j * t, t)
            return _online_softmax_step(q, k_ref[pl.ds(k0, t), :], v_ref[pl.ds(k0, t), :],
                                        carry, None)

        init = (jnp.full((t, 1), -jnp.inf, F32), jnp.zeros((t, 1), F32),
                jnp.zeros((t, V_DIM), F32))
        carry = lax.fori_loop(0, qi, kv, init)
        _, l, acc = _online_softmax_step(q, k_ref[pl.ds(r0, t), :], v_ref[pl.ds(r0, t), :],
                                         carry, diag_valid)
        o_ref[pl.ds(r0, t), :] = (acc / l).astype(o_ref.dtype)
        return c

    lax.fori_loop(0, nq, q_tile, 0)


def _latent_attention(q, k, v, b, s):
    t = min(256, s)
    assert s % t == 0 and t % CHUNK == 0
    qk_spec = pl.BlockSpec((None, s, PADDED_HEAD), lambda bi, h: (bi, 0, h))
    v_spec = pl.BlockSpec((None, s, V_DIM), lambda bi, h: (bi, 0, h))
    out = pl.pallas_call(
        functools.partial(_mla_kernel, nq=s // t, t=t),
        out_shape=jax.ShapeDtypeStruct((b, s, H_C * V_DIM), BF16),
        grid=(b, H_C),
        in_specs=[qk_spec, qk_spec, v_spec],
        out_specs=v_spec,
        compiler_params=_params("arbitrary", "arbitrary"),
        name="latent_attn",
    )(q.reshape(b, s, H_C * PADDED_HEAD), k.reshape(b, s, H_C * PADDED_HEAD),
      v.reshape(b, s, H_C * V_DIM))
    return out.reshape(b * s, H_C * V_DIM)


def _mla_sample_kernel(q_ref, k_ref, v_ref, o_ref, *, n_valid, qpos0):
    t, tk = q_ref.shape[0], k_ref.shape[0]
    row = lax.broadcasted_iota(jnp.int32, (t, tk), 0)
    col = lax.broadcasted_iota(jnp.int32, (t, tk), 1)
    valid = jnp.logical_and(col < n_valid, _chunk_of(col) <= _chunk_of(qpos0 + row))
    bias = jnp.where(valid, 0.0, -jnp.inf).astype(F32)
    o_ref[...] = _softmax_attend(q_ref[...], k_ref[...], v_ref[...], bias).astype(o_ref.dtype)


def _latent_attention_sample(q, k, v, bs, t, tk, n_valid, qpos0):
    return pl.pallas_call(
        functools.partial(_mla_sample_kernel, n_valid=n_valid, qpos0=qpos0),
        out_shape=jax.ShapeDtypeStruct((bs * t, H_C * V_DIM), BF16),
        grid=(bs, H_C),
        in_specs=[pl.BlockSpec((t, PADDED_HEAD), lambda bi, h: (bi, h)),
                  pl.BlockSpec((None, tk, PADDED_HEAD), lambda bi, h: (bi, 0, h)),
                  pl.BlockSpec((None, tk, V_DIM), lambda bi, h: (bi, 0, h))],
        out_specs=pl.BlockSpec((t, V_DIM), lambda bi, h: (bi, h)),
        compiler_params=_params("arbitrary", "arbitrary"),
        name="latent_attn_sample",
    )(q, k, v)


def _rope_tables(pos):
    half = ROPE_DIM // 2
    inv = ROPE_THETA ** (-jnp.arange(half, dtype=F32) / half)
    ang = pos.astype(F32)[:, None] * inv[None, :]
    cos, sin = jnp.cos(ang), jnp.sin(ang)
    z = jnp.zeros((pos.shape[0], LANES - ROPE_DIM), F32)
    return (jnp.concatenate([cos, cos, z], -1), jnp.concatenate([-sin, sin, z], -1))


def _swap_halves(w):
    half = ROPE_DIM // 2
    return jnp.concatenate([w[..., half:], w[..., :half]], -1)


def _pad_lanes(g):
    return jnp.concatenate([g, jnp.zeros((LANES - g.shape[0],), g.dtype)])[None, :]


class _Stream:
    def __init__(self, x, row0, pos0):
        self.b, self.s, self.d = x.shape
        self.m = self.b * self.s
        self.row0 = row0
        self.pos0 = pos0
        self.per_row_gate = self.s < 128


def _gate_operand(st, mod, ls):
    d = st.d
    if st.per_row_gate:
        g = jnp.repeat(mod[ls, st.row0:st.row0 + st.b, 2 * d:], st.s, axis=0)
        return g, lambda tm, tn: pl.BlockSpec((tm, tn), lambda i, j, kk: (i, j))
    mod4 = mod.reshape(mod.shape[0], mod.shape[1], 1, 3 * d)

    def spec(tm, tn):
        per_b = st.s // tm
        return pl.BlockSpec((None, None, 1, tn),
                            lambda i, j, kk: (ls, st.row0 + i // per_b, 0, 2 * d // tn + j))
    return mod4, spec


def _mlp_sublayer(st, x2d, mod, norm_g4, ls, w1, w2):
    d = st.d
    mod4 = mod.reshape(mod.shape[0], mod.shape[1], 1, 3 * d)
    h = _adanorm(x2d.reshape(st.b, st.s, d), norm_g4, mod4, ls, st.row0)
    dff = w1.shape[1]
    tm = _row_tile(st.m)
    tn = min(512, dff)
    a, = _matmul(h, w1, 0, dff, tn, _epi_relu2, [],
                 [((st.m, dff), BF16, (tm, tn), lambda i, j: (i, j))], "mlp_up")
    gate, gate_spec = _gate_operand(st, mod, ls)
    return _resid_matmul(a, w2, x2d, gate, gate_spec, "mlp_down")


def _even_layer(st, x2d, mod, norm_g4, ls, w_in, g_q, g_k, bias, w_out, cache):
    d = st.d
    m = st.m
    mod4 = mod.reshape(mod.shape[0], mod.shape[1], 1, 3 * d)
    h = _adanorm(x2d.reshape(st.b, st.s, d), norm_g4, mod4, ls, st.row0)
    tm = _row_tile(m)
    wa = H_A * HEAD_DIM
    wb = H_B * HEAD_DIM
    d_ab = wa + wb
    tn = 512
    scale = HEAD_DIM ** -0.5
    gspec = (pl.BlockSpec((1, HEAD_DIM), lambda i, j: (0, 0)))
    blk = lambda i, j: (i, j)
    bf = lambda: ((m, wa), BF16, (tm, tn), blk)
    f32 = lambda: ((m, wa), F32, (tm, tn), blk)

    def gain(g):
        return [(g[None, :], (1, HEAD_DIM), lambda i, j: (0, 0))]

    qa, = _matmul(h, w_in, 0, wa, tn, functools.partial(_epi_headnorm, scale=scale), gain(g_q),
                  [bf()], "proj_qa")
    qb, = _matmul(h, w_in, wa, wb, tn, functools.partial(_epi_plain, scale=scale), [],
                  [bf()], "proj_qb")
    ka, ka32 = _matmul(h, w_in, d_ab, wa, tn, _epi_headnorm, gain(g_k), [bf(), f32()], "proj_ka")
    kb, kb32 = _matmul(h, w_in, d_ab + wa, wb, tn, _epi_plain, [], [bf(), f32()], "proj_kb")
    va, va32 = _matmul(h, w_in, 2 * d_ab, wa, tn, _epi_plain, [], [bf(), f32()], "proj_va")
    vb, vb32 = _matmul(h, w_in, 2 * d_ab + wa, wb, tn, _epi_plain, [], [bf(), f32()], "proj_vb")
    del gspec

    if cache is None:
        oa = _band_attention(qa, ka, va, bias, st.b, st.s)
        t = min(MXU_DIM, st.s)
        v3 = lambda a: a.reshape(st.b, st.s, wb)
        ob = _stick_breaking(v3(qb), v3(kb), v3(vb), st.b, st.s, st.s, t, t, 0).reshape(m, wb)
    else:
        ca_k, ca_v, cb_k, cb_v = cache
        n_keep = ca_k.shape[1]
        assert st.s <= CHUNK and n_keep + st.s <= BAND_W and n_keep == BAND_PAST

        def with_cache(c, new, total):
            c = c.reshape(st.b, c.shape[1], -1).astype(BF16)
            new = new.reshape(st.b, st.s, -1)
            pad = jnp.zeros((st.b, total - c.shape[1] - st.s, c.shape[2]), BF16)
            return jnp.concatenate([c, new, pad], 1)

        oa = _band_attention_sample(qa, with_cache(ca_k, ka, BAND_W), with_cache(ca_v, va, BAND_W),
                                    bias, st.b, st.s, n_keep + st.s)
        past = cb_k.shape[1]
        tk = MXU_DIM
        assert past % tk == 0
        total = past + tk
        ob = _stick_breaking(qb.reshape(st.b, st.s, wb), with_cache(cb_k, kb, total),
                             with_cache(cb_v, vb, total), st.b, st.s, total, st.s, tk, past)
        ob = ob.reshape(m, wb)

    mix_in = jnp.concatenate([oa, ob], axis=1)
    gate, gate_spec = _gate_operand(st, mod, ls)
    x2d = _resid_matmul(mix_in, w_out, x2d, gate, gate_spec, "out_ab")
    r4 = lambda a, hh: a.reshape(st.b, st.s, hh, HEAD_DIM)
    return x2d, (r4(ka32, H_A), r4(va32, H_A), r4(kb32, H_B), r4(vb32, H_B))


def _odd_layer(st, x2d, mod, norm_g4, ls, wq_a, g_q_lat, wq_b, wkv_c, wkv_pe, g_kv_lat, wkv_bk, wkv_bv,
               g_q, g_k, w_out, cache):
    d = st.d
    m = st.m
    mod4 = mod.reshape(mod.shape[0], mod.shape[1], 1, 3 * d)
    h = _adanorm(x2d.reshape(st.b, st.s, d), norm_g4, mod4, ls, st.row0)
    tm = _row_tile(m)
    blk = lambda i, j: (i, j)
    row_blk = lambda i, j: (i, 0)
    one = lambda i, j: (0, 0)
    q_lora = wq_a.shape[1]
    kv_lora = wkv_c.shape[1]

    cos, sin = _rope_tables(st.pos0 + jnp.arange(st.s))
    per_b = st.s // tm if st.s >= tm else 0
    if per_b:
        tab_blk = lambda i, j: (i % per_b, 0)
    else:
        cos, sin = jnp.tile(cos, (st.b, 1)), jnp.tile(sin, (st.b, 1))
        tab_blk = row_blk
    tables = [(cos, (tm, LANES), tab_blk), (sin, (tm, LANES), tab_blk)]

    q_lat, = _matmul(h, wq_a, 0, q_lora, q_lora, _epi_rownorm,
                     [(g_q_lat[None, :], (1, q_lora), one)],
                     [((m, q_lora), BF16, (tm, q_lora), blk)], "proj_q_lat")
    c_kv, c_kv32 = _matmul(h, wkv_c, 0, kv_lora, kv_lora, _epi_rownorm,
                           [(g_kv_lat[None, :], (1, kv_lora), one)],
                           [((m, kv_lora), BF16, (tm, kv_lora), blk),
                            ((m, kv_lora), F32, (tm, kv_lora), blk)], "proj_c_kv")
    k_pe, = _matmul(h, wkv_pe, 0, LANES, LANES, _epi_rope, tables,
                    [((m, LANES), F32, (tm, LANES), blk)], "proj_k_pe")

    wide = H_C * PADDED_HEAD
    gq_n, gq_r = g_q[None, :NOPE_DIM], _pad_lanes(g_q[NOPE_DIM:])
    gk_n, gk_r = g_k[None, :NOPE_DIM], _pad_lanes(g_k[NOPE_DIM:])
    lane_vec = lambda g: (g, (1, LANES), one)
    q, = _matmul(q_lat, wq_b, 0, wide, 512,
                 functools.partial(_epi_q_latent, scale=QK_DIM ** -0.5),
                 tables + [lane_vec(gq_n), lane_vec(gq_r)],
                 [((m, wide), BF16, (tm, 512), blk)], "proj_q")

    if cache is None:
        c_all, pe_all, mk = c_kv, k_pe, m
    else:
        cc_kv, cc_pe = cache
        past = cc_kv.shape[1]
        total = past + MXU_DIM
        pad = total - past - st.s
        c_all = jnp.concatenate([cc_kv.astype(BF16), c_kv.reshape(st.b, st.s, kv_lora),
                                 jnp.zeros((st.b, pad, kv_lora), BF16)], 1)
        pe_c = jnp.concatenate([cc_pe, jnp.zeros(cc_pe.shape[:2] + (LANES - ROPE_DIM,), F32)], -1)
        pe_all = jnp.concatenate([pe_c, k_pe.reshape(st.b, st.s, LANES),
                                  jnp.zeros((st.b, pad, LANES), F32)], 1)
        mk = st.b * total
        c_all, pe_all = c_all.reshape(mk, kv_lora), pe_all.reshape(mk, LANES)

    tmk = _row_tile(mk)
    nk = H_C * NOPE_DIM
    k, = _matmul(c_all, wkv_bk, 0, nk, 512, _epi_k_latent,
                 [(pe_all, (tmk, LANES), row_blk), lane_vec(gk_n), lane_vec(gk_r)],
                 [((mk, wide), BF16, (tmk, 2 * 512), blk)], "proj_k")
    v, = _matmul(c_all, wkv_bv, 0, H_C * V_DIM, 512, _epi_plain, [],
                 [((mk, H_C * V_DIM), BF16, (tmk, 512), blk)], "proj_v")

    if cache is None:
        o = _latent_attention(q, k, v, st.b, st.s)
    else:
        o = _latent_attention_sample(q, k.reshape(st.b, total, wide),
                                     v.reshape(st.b, total, H_C * V_DIM),
                                     st.b, st.s, total, past + st.s, st.pos0)
    gate, gate_spec = _gate_operand(st, mod, ls)
    x2d = _resid_matmul(o, w_out, x2d, gate, gate_spec, "out_c")
    return x2d, (c_kv32.reshape(st.b, st.s, kv_lora), k_pe[:, :ROPE_DIM].reshape(st.b, st.s, ROPE_DIM))


def kernel(x_prompt, x_sample, cache_a_k, cache_a_v, cache_b_k, cache_b_v, cache_c_kv, cache_c_pe,
           c_prompt, c_sample, norm_g, ada_w, ada_b, w_in_ab, g_q_a, g_k_a, rel_bias_a, w_out_ab,
           wq_a_c, g_q_lat_c, wq_b_c, wkv_a_c, g_kv_lat_c, wkv_b_c, g_q_c, g_k_c, w_out_c,
           mlp_w1, mlp_w2):
    bp, s, d = x_prompt.shape
    bs, t, _ = x_sample.shape
    depth = norm_g.shape[0]
    past_len = cache_b_k.shape[2]
    n_keep = cache_a_k.shape[2]
    assert s >= n_keep

    n_rows = -(-(bp + bs) // 16) * 16
    c_rows = jnp.concatenate([c_prompt, c_sample, jnp.zeros((n_rows - bp - bs, d), F32)], 0)
    mod = _ada_params(c_rows, ada_w.reshape(depth * 2, d, 3 * d), ada_b.reshape(depth * 2, 1, 3 * d))
    norm_g4 = norm_g.reshape(depth * 2, 1, d)

    prompt = _Stream(x_prompt, 0, 0)
    sample = _Stream(x_sample, bp, past_len)
    xp = x_prompt.reshape(bp * s, d)
    xs = x_sample.reshape(bs * t, d)

    outs_p = {"a": [], "c": []}
    outs_s = {"a": [], "c": []}
    for layer in range(depth):
        ls = 2 * layer
        if layer % 2 == 0:
            e = layer // 2
            w_in = w_in_ab[e].astype(BF16)
            w_out = w_out_ab[e].astype(BF16)
            bias = _band_bias(rel_bias_a[e])
            xp, keep = _even_layer(prompt, xp, mod, norm_g4, ls, w_in, g_q_a[e], g_k_a[e], bias,
                                   w_out, None)
            outs_p["a"].append(keep)
            xs, keep = _even_layer(sample, xs, mod, norm_g4, ls, w_in, g_q_a[e], g_k_a[e], bias,
                                   w_out, (cache_a_k[e], cache_a_v[e], cache_b_k[e], cache_b_v[e]))
            outs_s["a"].append(keep)
        else:
            mi = layer // 2
            kv_lora = g_kv_lat_c.shape[1]
            wq_a = wq_a_c[mi].astype(BF16)
            wq_b = wq_b_c[mi].reshape(-1, H_C, QK_DIM)
            wq_b = jnp.concatenate([wq_b, _swap_halves(wq_b[..., NOPE_DIM:])], -1)
            wq_b = wq_b.reshape(-1, H_C * PADDED_HEAD).astype(BF16)
            wkv_a = wkv_a_c[mi]
            wkv_c = wkv_a[:, :kv_lora].astype(BF16)
            wkv_pe = wkv_a[:, kv_lora:]
            wkv_pe = jnp.concatenate([wkv_pe, _swap_halves(wkv_pe)], -1).astype(BF16)
            wkv_b = wkv_b_c[mi].reshape(kv_lora, H_C, NOPE_DIM + V_DIM)
            wkv_bk = wkv_b[..., :NOPE_DIM].reshape(kv_lora, H_C * NOPE_DIM).astype(BF16)
            wkv_bv = wkv_b[..., NOPE_DIM:].reshape(kv_lora, H_C * V_DIM).astype(BF16)
            w_out = w_out_c[mi].astype(BF16)
            args = (wq_a, g_q_lat_c[mi], wq_b, wkv_c, wkv_pe, g_kv_lat_c[mi], wkv_bk, wkv_bv,
                    g_q_c[mi], g_k_c[mi], w_out)
            xp, keep = _odd_layer(prompt, xp, mod, norm_g4, ls, *args, None)
            outs_p["c"].append(keep)
            xs, keep = _odd_layer(sample, xs, mod, norm_g4, ls, *args,
                                  (cache_c_kv[mi], cache_c_pe[mi]))
            outs_s["c"].append(keep)
        w1 = mlp_w1[layer].astype(BF16)
        w2 = mlp_w2[layer].astype(BF16)
        xp = _mlp_sublayer(prompt, xp, mod, norm_g4, ls + 1, w1, w2)
        xs = _mlp_sublayer(sample, xs, mod, norm_g4, ls + 1, w1, w2)

    stack = lambda items, i: jnp.stack([it[i] for it in items])
    tail = lambda a: a[:, :, s - n_keep:]
    return (xp.reshape(bp, s, d), xs.reshape(bs, t, d),
            tail(stack(outs_p["a"], 0)), tail(stack(outs_p["a"], 1)),
            stack(outs_s["a"], 0), stack(outs_s["a"], 1),
            stack(outs_p["a"], 2), stack(outs_p["a"], 3),
            stack(outs_s["a"], 2), stack(outs_s["a"], 3),
            stack(outs_p["c"], 0), stack(outs_p["c"], 1),
            stack(outs_s["c"], 0), stack(outs_s["c"], 1))
```

```python
import functools
import math

import jax
import jax.numpy as jnp
from jax import lax
from jax.experimental import pallas as pl
from jax.experimental.pallas import tpu as pltpu

CHUNK = 64
BAND_CHUNKS = 8
BAND_PAST = BAND_CHUNKS * CHUNK
HEAD_DIM = 128
H_A = 16
H_B = 16
REL_CLIP = 128
N_REL = 2 * REL_CLIP + 1
H_C = 32
NOPE_DIM = 128
ROPE_DIM = 64
QK_DIM = NOPE_DIM + ROPE_DIM
V_DIM = 128
ROPE_THETA = 10000.0
EPS = 1e-6

LANES = 128
MXU_DIM = 256
VMEM_LIMIT_BYTES = 52 * 1024 * 1024

SB_EXIT = 152.0
SB_GROUP = 4

BAND_Q = 2 * CHUNK
BAND_W = BAND_PAST + BAND_Q
BAND_UNROLL = 10
PADDED_HEAD = 2 * LANES

LOG2E = math.log2(math.e)
F32 = jnp.float32
BF16 = jnp.bfloat16
_NT = (((1,), (1,)), ((), ()))


def _chunk_of(pos):
    return jnp.right_shift(pos, int(math.log2(CHUNK)))


def _params(*sem):
    return pltpu.CompilerParams(dimension_semantics=sem, vmem_limit_bytes=VMEM_LIMIT_BYTES)


def _row_tile(m, cap=1024):
    t = min(m, cap)
    while m % t:
        t //= 2
    return t


def _ada_kernel(c_ref, w_ref, b_ref, o_ref, *, d_tiles):
    j = pl.program_id(1)
    c = c_ref[...]
    a = (c * jax.nn.sigmoid(c)).astype(BF16)
    acc = jnp.dot(a, w_ref[...].astype(BF16), preferred_element_type=F32) + b_ref[...]
    is_scale = jnp.logical_and(j >= d_tiles, j < 2 * d_tiles)
    o_ref[...] = acc + jnp.where(is_scale, 1.0, 0.0).astype(F32)


def _ada_params(c_rows, ada_w, ada_b):
    n_l, d, d3 = ada_w.shape
    r = c_rows.shape[0]
    tn = min(512, d)
    return pl.pallas_call(
        functools.partial(_ada_kernel, d_tiles=d // tn),
        out_shape=jax.ShapeDtypeStruct((n_l, r, d3), F32),
        grid=(n_l, d3 // tn),
        in_specs=[pl.BlockSpec((r, d), lambda l, j: (0, 0)),
                  pl.BlockSpec((None, d, tn), lambda l, j: (l, 0, j)),
                  pl.BlockSpec((None, 1, tn), lambda l, j: (l, 0, j))],
        out_specs=pl.BlockSpec((None, r, tn), lambda l, j: (l, 0, j)),
        compiler_params=_params("arbitrary", "arbitrary"),
        name="ada_params",
    )(c_rows, ada_w, ada_b)


def _adanorm_kernel(x_ref, g_ref, sh_ref, sc_ref, o_ref):
    x = x_ref[...]
    ms = jnp.mean(x * x, axis=-1, keepdims=True)
    y = x * lax.rsqrt(ms + EPS) * g_ref[...]
    o_ref[...] = (y * sc_ref[...] + sh_ref[...]).astype(o_ref.dtype)


def _adanorm(x, g, mod4, ls, row0):
    b, s, d = x.shape
    tm = _row_tile(s, 512)
    nt = s // tm
    return pl.pallas_call(
        _adanorm_kernel,
        out_shape=jax.ShapeDtypeStruct((b * s, d), BF16),
        grid=(b, nt),
        in_specs=[pl.BlockSpec((None, tm, d), lambda bi, i: (bi, i, 0)),
                  pl.BlockSpec((None, 1, d), lambda bi, i: (ls, 0, 0)),
                  pl.BlockSpec((None, None, 1, d), lambda bi, i: (ls, row0 + bi, 0, 0)),
                  pl.BlockSpec((None, None, 1, d), lambda bi, i: (ls, row0 + bi, 0, 1))],
        out_specs=pl.BlockSpec((tm, d), lambda bi, i: (bi * nt + i, 0)),
        compiler_params=_params("arbitrary", "arbitrary"),
        name="adanorm",
    )(x, g, mod4, mod4)


def _mm_kernel(*refs, n_extra, epilogue):
    x_ref, w_ref = refs[:2]
    extra = refs[2:2 + n_extra]
    outs = refs[2 + n_extra:]
    acc = jnp.dot(x_ref[...], w_ref[...], preferred_element_type=F32)
    epilogue(acc, extra, outs)


def _matmul(x, w, col0, ncols, tn, epilogue, extras, outs, name, layer=0):
    m, k = x.shape
    tm = _row_tile(m)
    assert ncols % tn == 0 and col0 % tn == 0
    j0 = col0 // tn
    if w.ndim == 3:
        w_spec = pl.BlockSpec((None, k, tn), lambda i, j: (layer, 0, j0 + j))
    else:
        w_spec = pl.BlockSpec((k, tn), lambda i, j: (0, j0 + j))
    in_specs = [pl.BlockSpec((tm, k), lambda i, j: (i, 0)), w_spec]
    in_specs += [pl.BlockSpec(bs, im) for _, bs, im in extras]
    return pl.pallas_call(
        functools.partial(_mm_kernel, n_extra=len(extras), epilogue=epilogue),
        out_shape=[jax.ShapeDtypeStruct(sh, dt) for sh, dt, _, _ in outs],
        grid=(m // tm, ncols // tn),
        in_specs=in_specs,
        out_specs=[pl.BlockSpec(bs, im) for _, _, bs, im in outs],
        compiler_params=_params("arbitrary", "arbitrary"),
        name=name,
    )(x, w, *[a for a, _, _ in extras])


def _epi_plain(acc, extra, outs, *, scale=1.0):
    for o in outs:
        o[...] = (acc * scale if scale != 1.0 else acc).astype(o.dtype)


def _epi_relu2(acc, extra, outs):
    outs[0][...] = jnp.square(jnp.maximum(acc, 0.0)).astype(outs[0].dtype)


def _epi_headnorm(acc, extra, outs, *, scale=1.0):
    g = extra[0][...]
    for h in range(acc.shape[1] // HEAD_DIM):
        sl = slice(h * HEAD_DIM, (h + 1) * HEAD_DIM)
        a = acc[:, sl]
        y = a * lax.rsqrt(jnp.mean(a * a, axis=-1, keepdims=True) + EPS) * g
        if scale != 1.0:
            y = y * scale
        for o in outs:
            o[:, sl] = y.astype(o.dtype)


def _epi_rownorm(acc, extra, outs):
    g = extra[0][...]
    y = acc * lax.rsqrt(jnp.mean(acc * acc, axis=-1, keepdims=True) + EPS) * g
    for o in outs:
        o[...] = y.astype(o.dtype)


def _epi_rope_shared(acc, extra, outs):
    cos, sin = extra[0][...], extra[1][...]
    a, b = acc[:, :LANES], acc[:, LANES:]
    lo = lax.broadcasted_iota(jnp.int32, a.shape, 1) < ROPE_DIM
    outs[0][:, :LANES] = jnp.where(lo, a * cos + b * sin, 0.0)
    outs[0][:, LANES:] = jnp.where(lo, 0.0, b * cos + a * sin)


def _epi_q_latent(acc, extra, outs, *, scale):
    cos, sin = extra[0][...], extra[1][...]
    gn, gr = extra[2][...], (extra[3][...], extra[4][...])
    pair = 2 * PADDED_HEAD
    lo = lax.broadcasted_iota(jnp.int32, cos.shape, 1) < ROPE_DIM
    for p in range(acc.shape[1] // pair):
        c0 = p * pair
        r = (acc[:, c0 + 2 * NOPE_DIM:c0 + 3 * NOPE_DIM] * cos
             + acc[:, c0 + 3 * NOPE_DIM:c0 + 4 * NOPE_DIM] * sin)
        r2 = r * r
        for h in range(2):
            n = acc[:, c0 + h * NOPE_DIM:c0 + (h + 1) * NOPE_DIM]
            mine = lo if h == 0 else jnp.logical_not(lo)
            ss = jnp.sum(n * n + jnp.where(mine, r2, 0.0), axis=-1, keepdims=True)
            inv = lax.rsqrt(ss * (1.0 / QK_DIM) + EPS) * scale
            o0 = c0 + h * PADDED_HEAD
            outs[0][:, o0:o0 + NOPE_DIM] = (n * inv * gn).astype(BF16)
            outs[0][:, o0 + NOPE_DIM:o0 + PADDED_HEAD] = (r * inv * gr[h]).astype(BF16)


def _epi_k_latent(acc, extra, outs):
    pe = (extra[0][:, :LANES], extra[0][:, LANES:])
    gn, gr = extra[1][...], (extra[2][...], extra[3][...])
    pe2 = pe[0] * pe[0]
    for h in range(acc.shape[1] // NOPE_DIM):
        n = acc[:, h * NOPE_DIM:(h + 1) * NOPE_DIM]
        ss = jnp.sum(n * n + pe2, axis=-1, keepdims=True)
        inv = lax.rsqrt(ss * (1.0 / QK_DIM) + EPS)
        c0 = h * PADDED_HEAD
        outs[0][:, c0:c0 + NOPE_DIM] = (n * inv * gn).astype(BF16)
        outs[0][:, c0 + NOPE_DIM:c0 + PADDED_HEAD] = (pe[h % 2] * inv * gr[h % 2]).astype(BF16)


def _resid_kernel(*refs, nk, n_a):
    a_refs = refs[:n_a]
    w_ref, x_ref, g_ref, o_ref = refs[n_a:n_a + 4]
    if nk == 1:
        acc = jnp.dot(a_refs[0][...], w_ref[...], preferred_element_type=F32)
        o_ref[...] = x_ref[...] + g_ref[...] * acc
        return
    acc_ref, = refs[n_a + 4:]
    k = pl.program_id(2)

    @pl.when(k == 0)
    def _():
        acc_ref[...] = jnp.zeros_like(acc_ref)

    if n_a == 1:
        acc_ref[...] += jnp.dot(a_refs[0][...], w_ref[...], preferred_element_type=F32)
    else:
        for idx, a_ref in enumerate(a_refs):
            @pl.when(k == idx)
            def _(a_ref=a_ref):
                acc_ref[...] += jnp.dot(a_ref[...], w_ref[...], preferred_element_type=F32)

    @pl.when(k == nk - 1)
    def _():
        o_ref[...] = x_ref[...] + g_ref[...] * acc_ref[...]


def _resid_matmul(a, w, x, gate, gate_spec, name, layer=0):
    a_list = list(a) if isinstance(a, (list, tuple)) else [a]
    m = a_list[0].shape[0]
    d = w.shape[-1]
    tm = _row_tile(m)
    tn = min(1024, d)
    if len(a_list) == 1:
        k = a_list[0].shape[1]
        tk = min(2048, k)
        nk = k // tk
        a_specs = [pl.BlockSpec((tm, tk), lambda i, j, kk: (i, kk))]
    else:
        tk = a_list[0].shape[1]
        nk = len(a_list)
        assert all(t.shape == (m, tk) for t in a_list)
        a_specs = [pl.BlockSpec((tm, tk), lambda i, j, kk: (i, 0)) for _ in a_list]
    scratch = [] if nk == 1 else [pltpu.VMEM((tm, tn), F32)]
    if w.ndim == 3:
        w_spec = pl.BlockSpec((None, tk, tn), lambda i, j, kk: (layer, kk, j))
    else:
        w_spec = pl.BlockSpec((tk, tn), lambda i, j, kk: (kk, j))
    return pl.pallas_call(
        functools.partial(_resid_kernel, nk=nk, n_a=len(a_list)),
        out_shape=jax.ShapeDtypeStruct((m, d), F32),
        grid=(m // tm, d // tn, nk),
        in_specs=a_specs + [w_spec,
                            pl.BlockSpec((tm, tn), lambda i, j, kk: (i, j)),
                            gate_spec(tm, tn)],
        out_specs=pl.BlockSpec((tm, tn), lambda i, j, kk: (i, j)),
        scratch_shapes=scratch,
        compiler_params=_params("arbitrary", "arbitrary", "arbitrary"),
        name=name,
    )(*a_list, w, x, gate)


def _bias_kernel(tab_ref, o_ref):
    h = pl.program_id(0)
    row = lax.broadcasted_iota(jnp.int32, (BAND_Q, BAND_W), 0)
    col = lax.broadcasted_iota(jnp.int32, (BAND_Q, BAND_W), 1)
    idx = jnp.clip(BAND_PAST + row - col, -REL_CLIP, REL_CLIP) + REL_CLIP
    rel = BAND_CHUNKS + _chunk_of(row) - _chunk_of(col)
    valid = jnp.logical_and(rel >= 0, rel <= BAND_CHUNKS)

    def body(t, acc):
        return jnp.where(idx == t, tab_ref[h, t], acc)

    acc = lax.fori_loop(0, N_REL, body, jnp.zeros((BAND_Q, BAND_W), F32))
    o_ref[...] = jnp.where(valid, acc * LOG2E, -jnp.inf)


def _band_bias(rel_bias):
    return pl.pallas_call(
        _bias_kernel,
        out_shape=jax.ShapeDtypeStruct((H_A, BAND_Q, BAND_W), F32),
        grid=(H_A,),
        in_specs=[pl.BlockSpec(memory_space=pltpu.SMEM)],
        out_specs=pl.BlockSpec((None, BAND_Q, BAND_W), lambda h: (h, 0, 0)),
        compiler_params=_params("arbitrary"),
        name="band_bias",
    )(rel_bias)


def _softmax_attend(q, k, v, bias):
    s = lax.dot_general(q, k, _NT, preferred_element_type=F32) + bias
    m = jnp.max(s, axis=-1, keepdims=True)
    p = jnp.exp2(s - m)
    l = jnp.sum(p, axis=-1, keepdims=True)
    o = jnp.dot(p.astype(BF16), v, preferred_element_type=F32)
    return o / l


def _band_kernel(q_ref, k_ref, v_ref, bias_ref, o_ref, *, nq):
    lead = BAND_PAST // BAND_Q
    for i in range(min(lead, nq)):
        rows = slice(i * BAND_Q, (i + 1) * BAND_Q)
        keys = slice(0, (i + 1) * BAND_Q)
        bias = bias_ref[:, (lead - i) * BAND_Q:]
        o_ref[rows, :] = _softmax_attend(q_ref[rows, :], k_ref[keys, :], v_ref[keys, :],
                                         bias).astype(o_ref.dtype)

    def body(i, c):
        r0 = pl.multiple_of(i * BAND_Q, BAND_Q)
        k0 = pl.multiple_of((i - lead) * BAND_Q, BAND_Q)
        o = _softmax_attend(q_ref[pl.ds(r0, BAND_Q), :], k_ref[pl.ds(k0, BAND_W), :],
                            v_ref[pl.ds(k0, BAND_W), :], bias_ref[...])
        o_ref[pl.ds(r0, BAND_Q), :] = o.astype(o_ref.dtype)
        return c

    if nq > lead:
        unroll = BAND_UNROLL if (nq - lead) % BAND_UNROLL == 0 else 1
        lax.fori_loop(lead, nq, body, 0, unroll=unroll)


def _band_attention(q, k, v, bias, b, s):
    assert s % BAND_Q == 0
    view = lambda t: t.reshape(b, s, H_A * HEAD_DIM)
    spec = pl.BlockSpec((None, s, HEAD_DIM), lambda bi, h: (bi, 0, h))
    out = pl.pallas_call(
        functools.partial(_band_kernel, nq=s // BAND_Q),
        out_shape=jax.ShapeDtypeStruct((b, s, H_A * HEAD_DIM), BF16),
        grid=(b, H_A),
        in_specs=[spec, spec, spec,
                  pl.BlockSpec((None, BAND_Q, BAND_W), lambda bi, h: (h, 0, 0))],
        out_specs=spec,
        compiler_params=_params("arbitrary", "arbitrary"),
        name="band_attn",
    )(view(q), view(k), view(v), bias)
    return out.reshape(b * s, H_A * HEAD_DIM)


def _band_sample_kernel(q_ref, k_ref, v_ref, bias_ref, o_ref, *, n_valid):
    t = q_ref.shape[0]
    col = lax.broadcasted_iota(jnp.int32, (t, BAND_W), 1)
    bias = jnp.where(col < n_valid, bias_ref[0:t, :], -jnp.inf)
    o_ref[...] = _softmax_attend(q_ref[...], k_ref[...], v_ref[...], bias).astype(o_ref.dtype)


def _band_attention_sample(q, kk, vv, bias, bs, t, n_valid):
    kv_spec = pl.BlockSpec((None, BAND_W, HEAD_DIM), lambda bi, h: (bi, 0, h))
    q_spec = pl.BlockSpec((t, HEAD_DIM), lambda bi, h: (bi, h))
    return pl.pallas_call(
        functools.partial(_band_sample_kernel, n_valid=n_valid),
        out_shape=jax.ShapeDtypeStruct((bs * t, H_A * HEAD_DIM), BF16),
        grid=(bs, H_A),
        in_specs=[q_spec, kv_spec, kv_spec,
                  pl.BlockSpec((None, BAND_Q, BAND_W), lambda bi, h: (h, 0, 0))],
        out_specs=q_spec,
        compiler_params=_params("arbitrary", "arbitrary"),
        name="band_attn_sample",
    )(q, kk, vv, bias)


def _sb_kernel(q_ref, k_ref, v_ref, o_ref, acc_ref, run_ref, *, nq, tq, tk, qoff, group):
    above = (lax.broadcasted_iota(jnp.int32, (tk, tk), 0)
             > lax.broadcasted_iota(jnp.int32, (tk, tk), 1)).astype(BF16)

    def block(q, j, run, causal):
        k0 = pl.multiple_of(j * tk, tk)
        z = lax.dot_general(q, k_ref[pl.ds(k0, tk), :], _NT, preferred_element_type=F32)
        log_sig = jnp.minimum(z, 0.0) - jnp.log2(1.0 + jnp.exp2(-jnp.abs(z)))
        log_keep = log_sig - z
        if causal is not None:
            log_keep = jnp.where(causal, log_keep, 0.0)
        after = jnp.dot(log_keep.astype(BF16), above, preferred_element_type=F32) + run
        w = jnp.exp2(log_sig + after)
        if causal is not None:
            w = jnp.where(causal, w, 0.0)
        o = jnp.dot(w.astype(BF16), v_ref[pl.ds(k0, tk), :], preferred_element_type=F32)
        return o, run + jnp.sum(log_keep, axis=-1, keepdims=True)

    row = lax.broadcasted_iota(jnp.int32, (tq, tk), 0)
    col = lax.broadcasted_iota(jnp.int32, (tq, tk), 1)

    def nearest_blocks(qi, u):
        r0 = pl.multiple_of(qi * tq, tq)
        q = q_ref[pl.ds(r0, tq), :]
        jd = (qoff + r0) // tk
        o, run = block(q, jd, jnp.zeros((tq, 1), F32), jd * tk + col < qoff + r0 + row)
        o_prev, run = block(q, jnp.maximum(jd - 1, 0), run, col < jnp.where(jd > 0, tk, 0))
        acc_ref[u] = o + o_prev
        run_ref[u] = run
        return jd, jnp.max(run)

    def remaining_blocks(qi, u, jd, run_max):
        r0 = pl.multiple_of(qi * tq, tq)

        def cond(st):
            j, run_max = st
            return jnp.logical_and(j >= 0, run_max > -SB_EXIT)

        def body(st):
            j, _ = st
            o, run = block(q_ref[pl.ds(r0, tq), :], j, run_ref[u], None)
            acc_ref[u] += o
            run_ref[u] = run
            return j - 1, jnp.max(run)

        lax.while_loop(cond, body, (jd - 2, run_max))
        o_ref[pl.ds(r0, tq), :] = acc_ref[u].astype(o_ref.dtype)

    def q_group(gi, c):
        state = [nearest_blocks(gi * group + u, u) for u in range(group)]
        for u, (jd, run_max) in enumerate(state):
            remaining_blocks(gi * group + u, u, jd, run_max)
        return c

    lax.fori_loop(0, nq // group, q_group, 0)


def _stick_breaking(q, k, v, b, tq_total, tk_total, tq, tk, qoff):
    assert tq_total % tq == 0 and tk_total % tk == 0 and tq <= tk and tk % tq == 0 and qoff % tk == 0
    assert qoff + tq_total <= tk_total
    w = H_B * HEAD_DIM
    q_spec = pl.BlockSpec((None, tq_total, HEAD_DIM), lambda bi, h: (bi, 0, h))
    kv_spec = pl.BlockSpec((None, tk_total, HEAD_DIM), lambda bi, h: (bi, 0, h))
    nq = tq_total // tq
    group = SB_GROUP if nq % SB_GROUP == 0 else 1
    return pl.pallas_call(
        functools.partial(_sb_kernel, nq=nq, tq=tq, tk=tk, qoff=qoff, group=group),
        out_shape=jax.ShapeDtypeStruct((b, tq_total, w), BF16),
        grid=(b, H_B),
        in_specs=[q_spec, kv_spec, kv_spec],
        out_specs=q_spec,
        scratch_shapes=[pltpu.VMEM((group, tq, HEAD_DIM), F32), pltpu.VMEM((group, tq, 1), F32)],
        compiler_params=_params("arbitrary", "arbitrary"),
        name="stick_breaking",
    )(q, k, v)


def _online_softmax_step(q, k, v, carry, valid):
    m, l, acc = carry
    s = lax.dot_general(q, k, _NT, preferred_element_type=F32)
    if valid is not None:
        s = jnp.where(valid, s, -jnp.inf)
    m_new = jnp.maximum(m, jnp.max(s, axis=-1, keepdims=True))
    alpha = jnp.exp2(m - m_new)
    p = jnp.exp2(s - m_new)
    l = alpha * l + jnp.sum(p, axis=-1, keepdims=True)
    acc = alpha * acc + jnp.dot(p.astype(BF16), v, preferred_element_type=F32)
    return m_new, l, acc


def _mla_kernel(q_ref, k_ref, v_ref, o_ref, *, nq, tq, tk):
    def q_tile(qi, c):
        r0 = pl.multiple_of(qi * tq, tq)
        q = q_ref[pl.ds(r0, tq), :]

        def kv(j, carry):
            k0 = pl.multiple_of(j * tk, tk)
            return _online_softmax_step(q, k_ref[pl.ds(k0, tk), :], v_ref[pl.ds(k0, tk), :],
                                        carry, None)

        carry = (jnp.full((tq, 1), -jnp.inf, F32), jnp.zeros((tq, 1), F32),
                 jnp.zeros((tq, V_DIM), F32))
        carry = lax.fori_loop(0, qi * (tq // tk), kv, carry)
        row = lax.broadcasted_iota(jnp.int32, (tq, tk), 0)
        col = lax.broadcasted_iota(jnp.int32, (tq, tk), 1)
        for d in range(tq // tk):
            k0 = pl.multiple_of(r0 + d * tk, tk)
            valid = _chunk_of(col + d * tk) <= _chunk_of(row)
            carry = _online_softmax_step(q, k_ref[pl.ds(k0, tk), :], v_ref[pl.ds(k0, tk), :],
                                         carry, valid)
        _, l, acc = carry
        o_ref[pl.ds(r0, tq), :] = (acc / l).astype(o_ref.dtype)
        return c

    lax.fori_loop(0, nq, q_tile, 0)


def _latent_attention(q, k, v, b, s, tq=1024, tk=1024):
    tq = min(tq, s)
    tk = min(tk, tq)
    assert s % tq == 0 and tq % tk == 0 and tk % CHUNK == 0
    qk_spec = pl.BlockSpec((None, s, PADDED_HEAD), lambda bi, h: (bi, 0, h))
    v_spec = pl.BlockSpec((None, s, V_DIM), lambda bi, h: (bi, 0, h))
    out = pl.pallas_call(
        functools.partial(_mla_kernel, nq=s // tq, tq=tq, tk=tk),
        out_shape=jax.ShapeDtypeStruct((b, s, H_C * V_DIM), BF16),
        grid=(b, H_C),
        in_specs=[qk_spec, qk_spec, v_spec],
        out_specs=v_spec,
        compiler_params=_params("arbitrary", "arbitrary"),
        name="latent_attn",
    )(q.reshape(b, s, H_C * PADDED_HEAD), k.reshape(b, s, H_C * PADDED_HEAD),
      v.reshape(b, s, H_C * V_DIM))
    return out.reshape(b * s, H_C * V_DIM)


def _mla_sample_kernel(q_ref, k_ref, v_ref, o_ref, *, n_valid, qpos0):
    t, tk = q_ref.shape[0], k_ref.shape[0]
    row = lax.broadcasted_iota(jnp.int32, (t, tk), 0)
    col = lax.broadcasted_iota(jnp.int32, (t, tk), 1)
    valid = jnp.logical_and(col < n_valid, _chunk_of(col) <= _chunk_of(qpos0 + row))
    bias = jnp.where(valid, 0.0, -jnp.inf).astype(F32)
    o_ref[...] = _softmax_attend(q_ref[...], k_ref[...], v_ref[...], bias).astype(o_ref.dtype)


def _latent_attention_sample(q, k, v, bs, t, tk, n_valid, qpos0):
    return pl.pallas_call(
        functools.partial(_mla_sample_kernel, n_valid=n_valid, qpos0=qpos0),
        out_shape=jax.ShapeDtypeStruct((bs * t, H_C * V_DIM), BF16),
        grid=(bs, H_C),
        in_specs=[pl.BlockSpec((t, PADDED_HEAD), lambda bi, h: (bi, h)),
                  pl.BlockSpec((None, tk, PADDED_HEAD), lambda bi, h: (bi, 0, h)),
                  pl.BlockSpec((None, tk, V_DIM), lambda bi, h: (bi, 0, h))],
        out_specs=pl.BlockSpec((t, V_DIM), lambda bi, h: (bi, h)),
        compiler_params=_params("arbitrary", "arbitrary"),
        name="latent_attn_sample",
    )(q, k, v)


def _rope_tables(pos):
    half = ROPE_DIM // 2
    inv = ROPE_THETA ** (-jnp.arange(half, dtype=F32) / half)
    ang = pos.astype(F32)[:, None] * inv[None, :]
    cos, sin = jnp.cos(ang), jnp.sin(ang)
    return (jnp.concatenate([cos, cos, cos, cos], -1), jnp.concatenate([-sin, sin, -sin, sin], -1))


def _swap_halves(w):
    half = ROPE_DIM // 2
    return jnp.concatenate([w[..., half:], w[..., :half]], -1)


def _rope_gains(g):
    z = jnp.zeros((LANES - g.shape[0],), g.dtype)
    return jnp.concatenate([g, z])[None, :], jnp.concatenate([z, g])[None, :]


class _Stream:
    def __init__(self, x, row0, pos0):
        self.b, self.s, self.d = x.shape
        self.m = self.b * self.s
        self.row0 = row0
        self.pos0 = pos0
        self.per_row_gate = self.s < 128


def _gate_operand(st, mod, ls):
    d = st.d
    if st.per_row_gate:
        g = jnp.repeat(mod[ls, st.row0:st.row0 + st.b, 2 * d:], st.s, axis=0)
        return g, lambda tm, tn: pl.BlockSpec((tm, tn), lambda i, j, kk: (i, j))
    mod4 = mod.reshape(mod.shape[0], mod.shape[1], 1, 3 * d)

    def spec(tm, tn):
        per_b = st.s // tm
        return pl.BlockSpec((None, None, 1, tn),
                            lambda i, j, kk: (ls, st.row0 + i // per_b, 0, 2 * d // tn + j))
    return mod4, spec


def _mlp_sublayer(st, x2d, mod, norm_g4, ls, w1, w2, layer):
    d = st.d
    mod4 = mod.reshape(mod.shape[0], mod.shape[1], 1, 3 * d)
    h = _adanorm(x2d.reshape(st.b, st.s, d), norm_g4, mod4, ls, st.row0)
    dff = w1.shape[-1]
    tm = _row_tile(st.m)
    tn = min(512, dff)
    a, = _matmul(h, w1, 0, dff, tn, _epi_relu2, [],
                 [((st.m, dff), BF16, (tm, tn), lambda i, j: (i, j))], "mlp_up", layer=layer)
    gate, gate_spec = _gate_operand(st, mod, ls)
    return _resid_matmul(a, w2, x2d, gate, gate_spec, "mlp_down", layer=layer)


def _even_layer(st, x2d, mod, norm_g4, ls, w_in, g_q, g_k, bias, w_out, cache):
    d = st.d
    m = st.m
    mod4 = mod.reshape(mod.shape[0], mod.shape[1], 1, 3 * d)
    h = _adanorm(x2d.reshape(st.b, st.s, d), norm_g4, mod4, ls, st.row0)
    tm = _row_tile(m)
    wa = H_A * HEAD_DIM
    wb = H_B * HEAD_DIM
    d_ab = wa + wb
    tn = 512
    scale = HEAD_DIM ** -0.5 * LOG2E
    blk = lambda i, j: (i, j)
    bf = lambda: ((m, wa), BF16, (tm, tn), blk)
    f32 = lambda: ((m, wa), F32, (tm, tn), blk)

    def gain(g):
        return [(g[None, :], (1, HEAD_DIM), lambda i, j: (0, 0))]

    qa, = _matmul(h, w_in, 0, wa, tn, functools.partial(_epi_headnorm, scale=scale), gain(g_q),
                  [bf()], "proj_qa")
    qb, = _matmul(h, w_in, wa, wb, tn, functools.partial(_epi_plain, scale=scale), [],
                  [bf()], "proj_qb")
    ka, ka32 = _matmul(h, w_in, d_ab, wa, tn, _epi_headnorm, gain(g_k), [bf(), f32()], "proj_ka")
    kb, kb32 = _matmul(h, w_in, d_ab + wa, wb, tn, _epi_plain, [], [bf(), f32()], "proj_kb")
    va, va32 = _matmul(h, w_in, 2 * d_ab, wa, tn, _epi_plain, [], [bf(), f32()], "proj_va")
    vb, vb32 = _matmul(h, w_in, 2 * d_ab + wa, wb, tn, _epi_plain, [], [bf(), f32()], "proj_vb")

    if cache is None:
        oa = _band_attention(qa, ka, va, bias, st.b, st.s)
        t = min(MXU_DIM, st.s)
        v3 = lambda a: a.reshape(st.b, st.s, wb)
        ob = _stick_breaking(v3(qb), v3(kb), v3(vb), st.b, st.s, st.s, t, t, 0).reshape(m, wb)
    else:
        ca_k, ca_v, cb_k, cb_v = cache
        n_keep = ca_k.shape[1]
        assert st.s <= CHUNK and n_keep + st.s <= BAND_W and n_keep == BAND_PAST

        def with_cache(c, new, total):
            c = c.reshape(st.b, c.shape[1], -1).astype(BF16)
            new = new.reshape(st.b, st.s, -1)
            pad = jnp.zeros((st.b, total - c.shape[1] - st.s, c.shape[2]), BF16)
            return jnp.concatenate([c, new, pad], 1)

        oa = _band_attention_sample(qa, with_cache(ca_k, ka, BAND_W), with_cache(ca_v, va, BAND_W),
                                    bias, st.b, st.s, n_keep + st.s)
        past = cb_k.shape[1]
        tk = MXU_DIM
        assert past % tk == 0
        total = past + tk
        ob = _stick_breaking(qb.reshape(st.b, st.s, wb), with_cache(cb_k, kb, total),
                             with_cache(cb_v, vb, total), st.b, st.s, total, st.s, tk, past)
        ob = ob.reshape(m, wb)

    gate, gate_spec = _gate_operand(st, mod, ls)
    x2d = _resid_matmul([oa, ob], w_out, x2d, gate, gate_spec, "out_ab")
    r4 = lambda a, hh: a.reshape(st.b, st.s, hh, HEAD_DIM)
    return x2d, (r4(ka32, H_A), r4(va32, H_A), r4(kb32, H_B), r4(vb32, H_B))


def _odd_layer(st, x2d, mod, norm_g4, ls, wq_a, g_q_lat, wq_b, wkv_c, wkv_pe, g_kv_lat, wkv_bk, wkv_bv,
               g_q, g_k, w_out, cache):
    d = st.d
    m = st.m
    mod4 = mod.reshape(mod.shape[0], mod.shape[1], 1, 3 * d)
    h = _adanorm(x2d.reshape(st.b, st.s, d), norm_g4, mod4, ls, st.row0)
    tm = _row_tile(m)
    blk = lambda i, j: (i, j)
    row_blk = lambda i, j: (i, 0)
    one = lambda i, j: (0, 0)
    q_lora = wq_a.shape[1]
    kv_lora = wkv_c.shape[1]

    cos, sin = _rope_tables(st.pos0 + jnp.arange(st.s))
    per_b = st.s // tm if st.s >= tm else 0
    if per_b:
        tab_blk = lambda i, j: (i % per_b, 0)
    else:
        cos, sin = jnp.tile(cos, (st.b, 1)), jnp.tile(sin, (st.b, 1))
        tab_blk = row_blk
    tables = [(cos, (tm, LANES), tab_blk), (sin, (tm, LANES), tab_blk)]

    q_lat, = _matmul(h, wq_a, 0, q_lora, q_lora, _epi_rownorm,
                     [(g_q_lat[None, :], (1, q_lora), one)],
                     [((m, q_lora), BF16, (tm, q_lora), blk)], "proj_q_lat")
    c_kv, c_kv32 = _matmul(h, wkv_c, 0, kv_lora, kv_lora, _epi_rownorm,
                           [(g_kv_lat[None, :], (1, kv_lora), one)],
                           [((m, kv_lora), BF16, (tm, kv_lora), blk),
                            ((m, kv_lora), F32, (tm, kv_lora), blk)], "proj_c_kv")
    k_pe, = _matmul(h, wkv_pe, 0, 2 * LANES, 2 * LANES, _epi_rope_shared, tables,
                    [((m, 2 * LANES), F32, (tm, 2 * LANES), blk)], "proj_k_pe")

    wide = H_C * PADDED_HEAD
    gq_n, gq_r = g_q[None, :NOPE_DIM], _rope_gains(g_q[NOPE_DIM:])
    gk_n, gk_r = g_k[None, :NOPE_DIM], _rope_gains(g_k[NOPE_DIM:])
    lane_vec = lambda g: (g, (1, LANES), one)
    q, = _matmul(q_lat, wq_b, 0, wide, 512,
                 functools.partial(_epi_q_latent, scale=QK_DIM ** -0.5 * LOG2E),
                 tables + [lane_vec(gq_n), lane_vec(gq_r[0]), lane_vec(gq_r[1])],
                 [((m, wide), BF16, (tm, 512), blk)], "proj_q")

    if cache is None:
        c_all, pe_all, mk = c_kv, k_pe, m
    else:
        cc_kv, cc_pe = cache
        past = cc_kv.shape[1]
        total = past + MXU_DIM
        pad = total - past - st.s
        c_all = jnp.concatenate([cc_kv.astype(BF16), c_kv.reshape(st.b, st.s, kv_lora),
                                 jnp.zeros((st.b, pad, kv_lora), BF16)], 1)
        gap = jnp.zeros(cc_pe.shape[:2] + (2 * (LANES - ROPE_DIM),), F32)
        pe_c = jnp.concatenate([cc_pe, gap, cc_pe], -1)
        pe_all = jnp.concatenate([pe_c, k_pe.reshape(st.b, st.s, 2 * LANES),
                                  jnp.zeros((st.b, pad, 2 * LANES), F32)], 1)
        mk = st.b * total
        c_all, pe_all = c_all.reshape(mk, kv_lora), pe_all.reshape(mk, 2 * LANES)

    tmk = _row_tile(mk)
    nk = H_C * NOPE_DIM
    k, = _matmul(c_all, wkv_bk, 0, nk, 512, _epi_k_latent,
                 [(pe_all, (tmk, 2 * LANES), row_blk), lane_vec(gk_n), lane_vec(gk_r[0]),
                  lane_vec(gk_r[1])],
                 [((mk, wide), BF16, (tmk, 2 * 512), blk)], "proj_k")
    v, = _matmul(c_all, wkv_bv, 0, H_C * V_DIM, 512, _epi_plain, [],
                 [((mk, H_C * V_DIM), BF16, (tmk, 512), blk)], "proj_v")

    if cache is None:
        o = _latent_attention(q, k, v, st.b, st.s)
    else:
        o = _latent_attention_sample(q, k.reshape(st.b, total, wide),
                                     v.reshape(st.b, total, H_C * V_DIM),
                                     st.b, st.s, total, past + st.s, st.pos0)
    gate, gate_spec = _gate_operand(st, mod, ls)
    x2d = _resid_matmul(o, w_out, x2d, gate, gate_spec, "out_c")
    return x2d, (c_kv32.reshape(st.b, st.s, kv_lora), k_pe[:, :ROPE_DIM].reshape(st.b, st.s, ROPE_DIM))


def kernel(x_prompt, x_sample, cache_a_k, cache_a_v, cache_b_k, cache_b_v, cache_c_kv, cache_c_pe,
           c_prompt, c_sample, norm_g, ada_w, ada_b, w_in_ab, g_q_a, g_k_a, rel_bias_a, w_out_ab,
           wq_a_c, g_q_lat_c, wq_b_c, wkv_a_c, g_kv_lat_c, wkv_b_c, g_q_c, g_k_c, w_out_c,
           mlp_w1, mlp_w2):
    bp, s, d = x_prompt.shape
    bs, t, _ = x_sample.shape
    depth = norm_g.shape[0]
    past_len = cache_b_k.shape[2]
    n_keep = cache_a_k.shape[2]
    assert s >= n_keep

    n_rows = -(-(bp + bs) // 16) * 16
    c_rows = jnp.concatenate([c_prompt, c_sample, jnp.zeros((n_rows - bp - bs, d), F32)], 0)
    mod = _ada_params(c_rows, ada_w.reshape(depth * 2, d, 3 * d), ada_b.reshape(depth * 2, 1, 3 * d))
    norm_g4 = norm_g.reshape(depth * 2, 1, d)
    w1_all, w2_all = mlp_w1.astype(BF16), mlp_w2.astype(BF16)

    prompt = _Stream(x_prompt, 0, 0)
    sample = _Stream(x_sample, bp, past_len)
    xp = x_prompt.reshape(bp * s, d)
    xs = x_sample.reshape(bs * t, d)

    outs_p = {"a": [], "c": []}
    outs_s = {"a": [], "c": []}
    for layer in range(depth):
        ls = 2 * layer
        if layer % 2 == 0:
            e = layer // 2
            w_in = w_in_ab[e].astype(BF16)
            w_out = w_out_ab[e].astype(BF16)
            bias = _band_bias(rel_bias_a[e])
            xp, keep = _even_layer(prompt, xp, mod, norm_g4, ls, w_in, g_q_a[e], g_k_a[e], bias,
                                   w_out, None)
            outs_p["a"].append(keep)
            xs, keep = _even_layer(sample, xs, mod, norm_g4, ls, w_in, g_q_a[e], g_k_a[e], bias,
                                   w_out, (cache_a_k[e], cache_a_v[e], cache_b_k[e], cache_b_v[e]))
            outs_s["a"].append(keep)
        else:
            mi = layer // 2
            kv_lora = g_kv_lat_c.shape[1]
            wq_a = wq_a_c[mi].astype(BF16)
            wq_b = wq_b_c[mi].astype(BF16).reshape(-1, H_C // 2, 2, QK_DIM)
            q_lora = wq_b.shape[0]
            wq_n = wq_b[..., :NOPE_DIM].reshape(q_lora, H_C // 2, 2 * NOPE_DIM)
            wq_r = wq_b[..., NOPE_DIM:]
            wq_b = jnp.concatenate([wq_n, wq_r.reshape(q_lora, H_C // 2, 2 * ROPE_DIM),
                                    _swap_halves(wq_r).reshape(q_lora, H_C // 2, 2 * ROPE_DIM)], -1)
            wq_b = wq_b.reshape(q_lora, H_C * PADDED_HEAD)
            wkv_a = wkv_a_c[mi]
            wkv_c = wkv_a[:, :kv_lora].astype(BF16)
            wkv_pe = wkv_a[:, kv_lora:].astype(BF16)
            wkv_sw = _swap_halves(wkv_pe)
            wkv_pe = jnp.concatenate([wkv_pe, wkv_sw, wkv_sw, wkv_pe], -1)
            wkv_b = wkv_b_c[mi].reshape(kv_lora, H_C, NOPE_DIM + V_DIM)
            wkv_bk = wkv_b[..., :NOPE_DIM].reshape(kv_lora, H_C * NOPE_DIM).astype(BF16)
            wkv_bv = wkv_b[..., NOPE_DIM:].reshape(kv_lora, H_C * V_DIM).astype(BF16)
            w_out = w_out_c[mi].astype(BF16)
            args = (wq_a, g_q_lat_c[mi], wq_b, wkv_c, wkv_pe, g_kv_lat_c[mi], wkv_bk, wkv_bv,
                    g_q_c[mi], g_k_c[mi], w_out)
            xp, keep = _odd_layer(prompt, xp, mod, norm_g4, ls, *args, None)
            outs_p["c"].append(keep)
            xs, keep = _odd_layer(sample, xs, mod, norm_g4, ls, *args,
                                  (cache_c_kv[mi], cache_c_pe[mi]))
            outs_s["c"].append(keep)
        xp = _mlp_sublayer(prompt, xp, mod, norm_g4, ls + 1, w1_all, w2_all, layer)
        xs = _mlp_sublayer(sample, xs, mod, norm_g4, ls + 1, w1_all, w2_all, layer)

    stack = lambda items, i: jnp.stack([it[i] for it in items])
    tail = lambda a: a[:, :, s - n_keep:]
    return (xp.reshape(bp, s, d), xs.reshape(bs, t, d),
            tail(stack(outs_p["a"], 0)), tail(stack(outs_p["a"], 1)),
            stack(outs_s["a"], 0), stack(outs_s["a"], 1),
            stack(outs_p["a"], 2), stack(outs_p["a"], 3),
            stack(outs_s["a"], 2), stack(outs_s["a"], 3),
            stack(outs_p["c"], 0), stack(outs_p["c"], 1),
            stack(outs_s["c"], 0), stack(outs_s["c"], 1))
```

```python
import functools
import math

import jax
import jax.numpy as jnp
from jax import lax
from jax.experimental import pallas as pl
from jax.experimental.pallas import tpu as pltpu

CHUNK = 64
BAND_CHUNKS = 8
BAND_PAST = BAND_CHUNKS * CHUNK
HEAD_DIM = 128
H_A = 16
H_B = 16
REL_CLIP = 128
N_REL = 2 * REL_CLIP + 1
H_C = 32
NOPE_DIM = 128
ROPE_DIM = 64
QK_DIM = NOPE_DIM + ROPE_DIM
V_DIM = 128
ROPE_THETA = 10000.0
EPS = 1e-6

LANES = 128
MXU_DIM = 256
VMEM_LIMIT_BYTES = 52 * 1024 * 1024

SB_EXIT = 152.0
SB_GROUP = 4

BAND_Q = 2 * CHUNK
BAND_W = BAND_PAST + BAND_Q
BAND_UNROLL = 10
PADDED_HEAD = 2 * LANES

LOG2E = math.log2(math.e)
F32 = jnp.float32
BF16 = jnp.bfloat16
_NT = (((1,), (1,)), ((), ()))


def _chunk_of(pos):
    return jnp.right_shift(pos, int(math.log2(CHUNK)))


def _params(*sem):
    return pltpu.CompilerParams(dimension_semantics=sem, vmem_limit_bytes=VMEM_LIMIT_BYTES)


def _row_tile(m, cap=1024):
    t = min(m, cap)
    while m % t:
        t //= 2
    return t


def _ada_kernel(c_ref, w_ref, b_ref, o_ref, *, d_tiles):
    j = pl.program_id(1)
    c = c_ref[...]
    a = (c * jax.nn.sigmoid(c)).astype(BF16)
    acc = jnp.dot(a, w_ref[...].astype(BF16), preferred_element_type=F32) + b_ref[...]
    is_scale = jnp.logical_and(j >= d_tiles, j < 2 * d_tiles)
    o_ref[...] = acc + jnp.where(is_scale, 1.0, 0.0).astype(F32)


def _ada_params(c_rows, ada_w, ada_b):
    n_l, d, d3 = ada_w.shape
    r = c_rows.shape[0]
    tn = min(512, d)
    return pl.pallas_call(
        functools.partial(_ada_kernel, d_tiles=d // tn),
        out_shape=jax.ShapeDtypeStruct((n_l, r, d3), F32),
        grid=(n_l, d3 // tn),
        in_specs=[pl.BlockSpec((r, d), lambda l, j: (0, 0)),
                  pl.BlockSpec((None, d, tn), lambda l, j: (l, 0, j)),
                  pl.BlockSpec((None, 1, tn), lambda l, j: (l, 0, j))],
        out_specs=pl.BlockSpec((None, r, tn), lambda l, j: (l, 0, j)),
        compiler_params=_params("arbitrary", "arbitrary"),
        name="ada_params",
    )(c_rows, ada_w, ada_b)


def _adanorm_kernel(x_ref, g_ref, sh_ref, sc_ref, o_ref):
    x = x_ref[...]
    ms = jnp.mean(x * x, axis=-1, keepdims=True)
    y = x * lax.rsqrt(ms + EPS) * g_ref[...]
    o_ref[...] = (y * sc_ref[...] + sh_ref[...]).astype(o_ref.dtype)


def _adanorm(x, g, mod4, ls, row0):
    b, s, d = x.shape
    tm = _row_tile(s, 512)
    nt = s // tm
    return pl.pallas_call(
        _adanorm_kernel,
        out_shape=jax.ShapeDtypeStruct((b * s, d), BF16),
        grid=(b, nt),
        in_specs=[pl.BlockSpec((None, tm, d), lambda bi, i: (bi, i, 0)),
                  pl.BlockSpec((None, 1, d), lambda bi, i: (ls, 0, 0)),
                  pl.BlockSpec((None, None, 1, d), lambda bi, i: (ls, row0 + bi, 0, 0)),
                  pl.BlockSpec((None, None, 1, d), lambda bi, i: (ls, row0 + bi, 0, 1))],
        out_specs=pl.BlockSpec((tm, d), lambda bi, i: (bi * nt + i, 0)),
        compiler_params=_params("arbitrary", "arbitrary"),
        name="adanorm",
    )(x, g, mod4, mod4)


def _mm_kernel(*refs, n_extra, epilogue):
    x_ref, w_ref = refs[:2]
    extra = refs[2:2 + n_extra]
    outs = refs[2 + n_extra:]
    acc = jnp.dot(x_ref[...], w_ref[...], preferred_element_type=F32)
    epilogue(acc, extra, outs)


def _matmul(x, w, col0, ncols, tn, epilogue, extras, outs, name, layer=0):
    m, k = x.shape
    tm = _row_tile(m)
    assert ncols % tn == 0 and col0 % tn == 0
    j0 = col0 // tn
    if w.ndim == 3:
        w_spec = pl.BlockSpec((None, k, tn), lambda i, j: (layer, 0, j0 + j))
    else:
        w_spec = pl.BlockSpec((k, tn), lambda i, j: (0, j0 + j))
    in_specs = [pl.BlockSpec((tm, k), lambda i, j: (i, 0)), w_spec]
    in_specs += [pl.BlockSpec(bs, im) for _, bs, im in extras]
    return pl.pallas_call(
        functools.partial(_mm_kernel, n_extra=len(extras), epilogue=epilogue),
        out_shape=[jax.ShapeDtypeStruct(sh, dt) for sh, dt, _, _ in outs],
        grid=(m // tm, ncols // tn),
        in_specs=in_specs,
        out_specs=[pl.BlockSpec(bs, im) for _, _, bs, im in outs],
        compiler_params=_params("arbitrary", "arbitrary"),
        name=name,
    )(x, w, *[a for a, _, _ in extras])


def _epi_plain(acc, extra, outs, *, scale=1.0):
    for o in outs:
        o[...] = (acc * scale if scale != 1.0 else acc).astype(o.dtype)


def _epi_relu2(acc, extra, outs):
    outs[0][...] = jnp.square(jnp.maximum(acc, 0.0)).astype(outs[0].dtype)


def _epi_headnorm(acc, extra, outs, *, scale=1.0):
    g = extra[0][...]
    for h in range(acc.shape[1] // HEAD_DIM):
        sl = slice(h * HEAD_DIM, (h + 1) * HEAD_DIM)
        a = acc[:, sl]
        y = a * lax.rsqrt(jnp.mean(a * a, axis=-1, keepdims=True) + EPS) * g
        if scale != 1.0:
            y = y * scale
        for o in outs:
            o[:, sl] = y.astype(o.dtype)


def _epi_rownorm(acc, extra, outs):
    g = extra[0][...]
    y = acc * lax.rsqrt(jnp.mean(acc * acc, axis=-1, keepdims=True) + EPS) * g
    for o in outs:
        o[...] = y.astype(o.dtype)


def _epi_rope_shared(acc, extra, outs):
    cos, sin = extra[0][...], extra[1][...]
    a, b = acc[:, :LANES], acc[:, LANES:]
    lo = lax.broadcasted_iota(jnp.int32, a.shape, 1) < ROPE_DIM
    outs[0][:, :LANES] = jnp.where(lo, a * cos + b * sin, 0.0)
    outs[0][:, LANES:] = jnp.where(lo, 0.0, b * cos + a * sin)


def _epi_q_latent(acc, extra, outs, *, scale):
    cos, sin = extra[0][...], extra[1][...]
    gn, gr = extra[2][...], (extra[3][...], extra[4][...])
    pair = 2 * PADDED_HEAD
    lo = lax.broadcasted_iota(jnp.int32, cos.shape, 1) < ROPE_DIM
    for p in range(acc.shape[1] // pair):
        c0 = p * pair
        r = (acc[:, c0 + 2 * NOPE_DIM:c0 + 3 * NOPE_DIM] * cos
             + acc[:, c0 + 3 * NOPE_DIM:c0 + 4 * NOPE_DIM] * sin)
        r2 = r * r
        for h in range(2):
            n = acc[:, c0 + h * NOPE_DIM:c0 + (h + 1) * NOPE_DIM]
            mine = lo if h == 0 else jnp.logical_not(lo)
            ss = jnp.sum(n * n + jnp.where(mine, r2, 0.0), axis=-1, keepdims=True)
            inv = lax.rsqrt(ss * (1.0 / QK_DIM) + EPS) * scale
            o0 = c0 + h * PADDED_HEAD
            outs[0][:, o0:o0 + NOPE_DIM] = (n * inv * gn).astype(BF16)
            outs[0][:, o0 + NOPE_DIM:o0 + PADDED_HEAD] = (r * inv * gr[h]).astype(BF16)


def _epi_k_latent(acc, extra, outs):
    pe = (extra[0][:, :LANES], extra[0][:, LANES:])
    gn, gr = extra[1][...], (extra[2][...], extra[3][...])
    pe2 = pe[0] * pe[0]
    for h in range(acc.shape[1] // NOPE_DIM):
        n = acc[:, h * NOPE_DIM:(h + 1) * NOPE_DIM]
        ss = jnp.sum(n * n + pe2, axis=-1, keepdims=True)
        inv = lax.rsqrt(ss * (1.0 / QK_DIM) + EPS)
        c0 = h * PADDED_HEAD
        outs[0][:, c0:c0 + NOPE_DIM] = (n * inv * gn).astype(BF16)
        outs[0][:, c0 + NOPE_DIM:c0 + PADDED_HEAD] = (pe[h % 2] * inv * gr[h % 2]).astype(BF16)


def _resid_kernel(*refs, nk, n_a):
    a_refs = refs[:n_a]
    w_ref, x_ref, g_ref, o_ref = refs[n_a:n_a + 4]
    if nk == 1:
        acc = jnp.dot(a_refs[0][...], w_ref[...], preferred_element_type=F32)
        o_ref[...] = x_ref[...] + g_ref[...] * acc
        return
    acc_ref, = refs[n_a + 4:]
    k = pl.program_id(2)

    @pl.when(k == 0)
    def _():
        acc_ref[...] = jnp.zeros_like(acc_ref)

    if n_a == 1:
        acc_ref[...] += jnp.dot(a_refs[0][...], w_ref[...], preferred_element_type=F32)
    else:
        for idx, a_ref in enumerate(a_refs):
            @pl.when(k == idx)
            def _(a_ref=a_ref):
                acc_ref[...] += jnp.dot(a_ref[...], w_ref[...], preferred_element_type=F32)

    @pl.when(k == nk - 1)
    def _():
        o_ref[...] = x_ref[...] + g_ref[...] * acc_ref[...]


def _resid_matmul(a, w, x, gate, gate_spec, name, layer=0):
    a_list = list(a) if isinstance(a, (list, tuple)) else [a]
    m = a_list[0].shape[0]
    d = w.shape[-1]
    tm = _row_tile(m)
    tn = min(1024, d)
    if len(a_list) == 1:
        k = a_list[0].shape[1]
        tk = min(2048, k)
        nk = k // tk
        a_specs = [pl.BlockSpec((tm, tk), lambda i, j, kk: (i, kk))]
    else:
        tk = a_list[0].shape[1]
        nk = len(a_list)
        assert all(t.shape == (m, tk) for t in a_list)
        a_specs = [pl.BlockSpec((tm, tk), lambda i, j, kk: (i, 0)) for _ in a_list]
    scratch = [] if nk == 1 else [pltpu.VMEM((tm, tn), F32)]
    if w.ndim == 3:
        w_spec = pl.BlockSpec((None, tk, tn), lambda i, j, kk: (layer, kk, j))
    else:
        w_spec = pl.BlockSpec((tk, tn), lambda i, j, kk: (kk, j))
    return pl.pallas_call(
        functools.partial(_resid_kernel, nk=nk, n_a=len(a_list)),
        out_shape=jax.ShapeDtypeStruct((m, d), F32),
        grid=(m // tm, d // tn, nk),
        in_specs=a_specs + [w_spec,
                            pl.BlockSpec((tm, tn), lambda i, j, kk: (i, j)),
                            gate_spec(tm, tn)],
        out_specs=pl.BlockSpec((tm, tn), lambda i, j, kk: (i, j)),
        scratch_shapes=scratch,
        compiler_params=_params("arbitrary", "arbitrary", "arbitrary"),
        name=name,
    )(*a_list, w, x, gate)


def _bias_kernel(tab_ref, o_ref):
    h = pl.program_id(0)
    row = lax.broadcasted_iota(jnp.int32, (BAND_Q, BAND_W), 0)
    col = lax.broadcasted_iota(jnp.int32, (BAND_Q, BAND_W), 1)
    idx = jnp.clip(BAND_PAST + row - col, -REL_CLIP, REL_CLIP) + REL_CLIP
    rel = BAND_CHUNKS + _chunk_of(row) - _chunk_of(col)
    valid = jnp.logical_and(rel >= 0, rel <= BAND_CHUNKS)

    def body(t, acc):
        return jnp.where(idx == t, tab_ref[h, t], acc)

    acc = lax.fori_loop(0, N_REL, body, jnp.zeros((BAND_Q, BAND_W), F32))
    o_ref[...] = jnp.where(valid, acc * LOG2E, -jnp.inf)


def _band_bias(rel_bias):
    return pl.pallas_call(
        _bias_kernel,
        out_shape=jax.ShapeDtypeStruct((H_A, BAND_Q, BAND_W), F32),
        grid=(H_A,),
        in_specs=[pl.BlockSpec(memory_space=pltpu.SMEM)],
        out_specs=pl.BlockSpec((None, BAND_Q, BAND_W), lambda h: (h, 0, 0)),
        compiler_params=_params("arbitrary"),
        name="band_bias",
    )(rel_bias)


def _softmax_attend(q, k, v, bias):
    s = lax.dot_general(q, k, _NT, preferred_element_type=F32) + bias
    m = jnp.max(s, axis=-1, keepdims=True)
    p = jnp.exp2(s - m)
    l = jnp.sum(p, axis=-1, keepdims=True)
    o = jnp.dot(p.astype(BF16), v, preferred_element_type=F32)
    return o / l


def _band_kernel(q_ref, k_ref, v_ref, bias_ref, o_ref, *, nq):
    lead = BAND_PAST // BAND_Q
    for i in range(min(lead, nq)):
        rows = slice(i * BAND_Q, (i + 1) * BAND_Q)
        keys = slice(0, (i + 1) * BAND_Q)
        bias = bias_ref[:, (lead - i) * BAND_Q:]
        o_ref[rows, :] = _softmax_attend(q_ref[rows, :], k_ref[keys, :], v_ref[keys, :],
                                         bias).astype(o_ref.dtype)

    def body(i, c):
        r0 = pl.multiple_of(i * BAND_Q, BAND_Q)
        k0 = pl.multiple_of((i - lead) * BAND_Q, BAND_Q)
        o = _softmax_attend(q_ref[pl.ds(r0, BAND_Q), :], k_ref[pl.ds(k0, BAND_W), :],
                            v_ref[pl.ds(k0, BAND_W), :], bias_ref[...])
        o_ref[pl.ds(r0, BAND_Q), :] = o.astype(o_ref.dtype)
        return c

    if nq > lead:
        unroll = BAND_UNROLL if (nq - lead) % BAND_UNROLL == 0 else 1
        lax.fori_loop(lead, nq, body, 0, unroll=unroll)


def _band_attention(q, k, v, bias, b, s):
    assert s % BAND_Q == 0
    view = lambda t: t.reshape(b, s, H_A * HEAD_DIM)
    spec = pl.BlockSpec((None, s, HEAD_DIM), lambda bi, h: (bi, 0, h))
    out = pl.pallas_call(
        functools.partial(_band_kernel, nq=s // BAND_Q),
        out_shape=jax.ShapeDtypeStruct((b, s, H_A * HEAD_DIM), BF16),
        grid=(b, H_A),
        in_specs=[spec, spec, spec,
                  pl.BlockSpec((None, BAND_Q, BAND_W), lambda bi, h: (h, 0, 0))],
        out_specs=spec,
        compiler_params=_params("arbitrary", "arbitrary"),
        name="band_attn",
    )(view(q), view(k), view(v), bias)
    return out.reshape(b * s, H_A * HEAD_DIM)


def _band_sample_kernel(q_ref, k_ref, v_ref, bias_ref, o_ref, *, n_valid):
    t = q_ref.shape[0]
    col = lax.broadcasted_iota(jnp.int32, (t, BAND_W), 1)
    bias = jnp.where(col < n_valid, bias_ref[0:t, :], -jnp.inf)
    o_ref[...] = _softmax_attend(q_ref[...], k_ref[...], v_ref[...], bias).astype(o_ref.dtype)


def _band_attention_sample(q, kk, vv, bias, bs, t, n_valid):
    kv_spec = pl.BlockSpec((None, BAND_W, HEAD_DIM), lambda bi, h: (bi, 0, h))
    q_spec = pl.BlockSpec((t, HEAD_DIM), lambda bi, h: (bi, h))
    return pl.pallas_call(
        functools.partial(_band_sample_kernel, n_valid=n_valid),
        out_shape=jax.ShapeDtypeStruct((bs * t, H_A * HEAD_DIM), BF16),
        grid=(bs, H_A),
        in_specs=[q_spec, kv_spec, kv_spec,
                  pl.BlockSpec((None, BAND_Q, BAND_W), lambda bi, h: (h, 0, 0))],
        out_specs=q_spec,
        compiler_params=_params("arbitrary", "arbitrary"),
        name="band_attn_sample",
    )(q, kk, vv, bias)


def _sb_kernel(q_ref, k_ref, v_ref, o_ref, acc_ref, run_ref, *, nq, tq, tk, qoff, group):
    above = (lax.broadcasted_iota(jnp.int32, (tk, tk), 0)
             > lax.broadcasted_iota(jnp.int32, (tk, tk), 1)).astype(BF16)

    def block(q, j, run, causal):
        k0 = pl.multiple_of(j * tk, tk)
        z = lax.dot_general(q, k_ref[pl.ds(k0, tk), :], _NT, preferred_element_type=F32)
        log_sig = jnp.minimum(z, 0.0) - jnp.log2(1.0 + jnp.exp2(-jnp.abs(z)))
        log_keep = log_sig - z
        if causal is not None:
            log_keep = jnp.where(causal, log_keep, 0.0)
        after = jnp.dot(log_keep.astype(BF16), above, preferred_element_type=F32) + run
        w = jnp.exp2(log_sig + after)
        if causal is not None:
            w = jnp.where(causal, w, 0.0)
        o = jnp.dot(w.astype(BF16), v_ref[pl.ds(k0, tk), :], preferred_element_type=F32)
        return o, run + jnp.sum(log_keep, axis=-1, keepdims=True)

    row = lax.broadcasted_iota(jnp.int32, (tq, tk), 0)
    col = lax.broadcasted_iota(jnp.int32, (tq, tk), 1)

    def nearest_blocks(qi, u):
        r0 = pl.multiple_of(qi * tq, tq)
        q = q_ref[pl.ds(r0, tq), :]
        jd = (qoff + r0) // tk
        o, run = block(q, jd, jnp.zeros((tq, 1), F32), jd * tk + col < qoff + r0 + row)
        o_prev, run = block(q, jnp.maximum(jd - 1, 0), run, col < jnp.where(jd > 0, tk, 0))
        acc_ref[u] = o + o_prev
        run_ref[u] = run
        return jd, jnp.max(run)

    def remaining_blocks(qi, u, jd, run_max):
        r0 = pl.multiple_of(qi * tq, tq)

        def cond(st):
            j, run_max = st
            return jnp.logical_and(j >= 0, run_max > -SB_EXIT)

        def body(st):
            j, _ = st
            o, run = block(q_ref[pl.ds(r0, tq), :], j, run_ref[u], None)
            acc_ref[u] += o
            run_ref[u] = run
            return j - 1, jnp.max(run)

        lax.while_loop(cond, body, (jd - 2, run_max))
        o_ref[pl.ds(r0, tq), :] = acc_ref[u].astype(o_ref.dtype)

    def q_group(gi, c):
        state = [nearest_blocks(gi * group + u, u) for u in range(group)]
        for u, (jd, run_max) in enumerate(state):
            remaining_blocks(gi * group + u, u, jd, run_max)
        return c

    lax.fori_loop(0, nq // group, q_group, 0)


def _stick_breaking(q, k, v, b, tq_total, tk_total, tq, tk, qoff):
    assert tq_total % tq == 0 and tk_total % tk == 0 and tq <= tk and tk % tq == 0 and qoff % tk == 0
    assert qoff + tq_total <= tk_total
    w = H_B * HEAD_DIM
    q_spec = pl.BlockSpec((None, tq_total, HEAD_DIM), lambda bi, h: (bi, 0, h))
    kv_spec = pl.BlockSpec((None, tk_total, HEAD_DIM), lambda bi, h: (bi, 0, h))
    nq = tq_total // tq
    group = SB_GROUP if nq % SB_GROUP == 0 else 1
    return pl.pallas_call(
        functools.partial(_sb_kernel, nq=nq, tq=tq, tk=tk, qoff=qoff, group=group),
        out_shape=jax.ShapeDtypeStruct((b, tq_total, w), BF16),
        grid=(b, H_B),
        in_specs=[q_spec, kv_spec, kv_spec],
        out_specs=q_spec,
        scratch_shapes=[pltpu.VMEM((group, tq, HEAD_DIM), F32), pltpu.VMEM((group, tq, 1), F32)],
        compiler_params=_params("arbitrary", "arbitrary"),
        name="stick_breaking",
    )(q, k, v)


def _online_softmax_step(q, k, v, carry, valid):
    m, l, acc = carry
    s = lax.dot_general(q, k, _NT, preferred_element_type=F32)
    if valid is not None:
        s = jnp.where(valid, s, -jnp.inf)
    m_new = jnp.maximum(m, jnp.max(s, axis=-1, keepdims=True))
    alpha = jnp.exp2(m - m_new)
    p = jnp.exp2(s - m_new)
    l = alpha * l + jnp.sum(p, axis=-1, keepdims=True)
    acc = alpha * acc + jnp.dot(p.astype(BF16), v, preferred_element_type=F32)
    return m_new, l, acc


def _mla_kernel(q_ref, k_ref, v_ref, o_ref, *, nq, tq, tk, td):
    row = lax.broadcasted_iota(jnp.int32, (td, td), 0)
    col = lax.broadcasted_iota(jnp.int32, (td, td), 1)
    diag_valid = _chunk_of(col) <= _chunk_of(row)

    def q_tile(qi, c):
        r0 = pl.multiple_of(qi * tq, tq)
        q = q_ref[pl.ds(r0, tq), :]

        def kv(j, carry):
            k0 = pl.multiple_of(j * tk, tk)
            return _online_softmax_step(q, k_ref[pl.ds(k0, tk), :], v_ref[pl.ds(k0, tk), :],
                                        carry, None)

        m, l, acc = lax.fori_loop(
            0, qi * (tq // tk), kv,
            (jnp.full((tq, 1), -jnp.inf, F32), jnp.zeros((tq, 1), F32), jnp.zeros((tq, V_DIM), F32)))
        for e in range(tq // td):
            rows = slice(e * td, (e + 1) * td)
            part = (m[rows], l[rows], acc[rows])
            qe = q[rows]
            if e > 0:
                k0 = pl.multiple_of(r0, td)
                part = _online_softmax_step(qe, k_ref[pl.ds(k0, e * td), :],
                                            v_ref[pl.ds(k0, e * td), :], part, None)
            k0 = pl.multiple_of(r0 + e * td, td)
            _, le, acce = _online_softmax_step(qe, k_ref[pl.ds(k0, td), :], v_ref[pl.ds(k0, td), :],
                                               part, diag_valid)
            o_ref[pl.ds(pl.multiple_of(r0 + e * td, td), td), :] = (acce / le).astype(o_ref.dtype)
        return c

    lax.fori_loop(0, nq, q_tile, 0)


def _latent_attention(q, k, v, b, s, tq=2048, tk=1024, td=512):
    tq = min(tq, s)
    tk, td = min(tk, tq), min(td, tq)
    assert s % tq == 0 and tq % tk == 0 and tq % td == 0 and td % CHUNK == 0
    qk_spec = pl.BlockSpec((None, s, PADDED_HEAD), lambda bi, h: (bi, 0, h))
    v_spec = pl.BlockSpec((None, s, V_DIM), lambda bi, h: (bi, 0, h))
    out = pl.pallas_call(
        functools.partial(_mla_kernel, nq=s // tq, tq=tq, tk=tk, td=td),
        out_shape=jax.ShapeDtypeStruct((b, s, H_C * V_DIM), BF16),
        grid=(b, H_C),
        in_specs=[qk_spec, qk_spec, v_spec],
        out_specs=v_spec,
        compiler_params=_params("arbitrary", "arbitrary"),
        name="latent_attn",
    )(q.reshape(b, s, H_C * PADDED_HEAD), k.reshape(b, s, H_C * PADDED_HEAD),
      v.reshape(b, s, H_C * V_DIM))
    return out.reshape(b * s, H_C * V_DIM)


def _mla_sample_kernel(q_ref, k_ref, v_ref, o_ref, *, n_valid, qpos0):
    t, tk = q_ref.shape[0], k_ref.shape[0]
    row = lax.broadcasted_iota(jnp.int32, (t, tk), 0)
    col = lax.broadcasted_iota(jnp.int32, (t, tk), 1)
    valid = jnp.logical_and(col < n_valid, _chunk_of(col) <= _chunk_of(qpos0 + row))
    bias = jnp.where(valid, 0.0, -jnp.inf).astype(F32)
    for g in range(q_ref.shape[1] // PADDED_HEAD):
        qk = slice(g * PADDED_HEAD, (g + 1) * PADDED_HEAD)
        vo = slice(g * V_DIM, (g + 1) * V_DIM)
        o_ref[:, vo] = _softmax_attend(q_ref[:, qk], k_ref[:, qk], v_ref[:, vo],
                                       bias).astype(o_ref.dtype)


def _latent_attention_sample(q, k, v, bs, t, tk, n_valid, qpos0, heads_per_step=4):
    g = heads_per_step
    return pl.pallas_call(
        functools.partial(_mla_sample_kernel, n_valid=n_valid, qpos0=qpos0),
        out_shape=jax.ShapeDtypeStruct((bs * t, H_C * V_DIM), BF16),
        grid=(bs, H_C // g),
        in_specs=[pl.BlockSpec((t, g * PADDED_HEAD), lambda bi, h: (bi, h)),
                  pl.BlockSpec((None, tk, g * PADDED_HEAD), lambda bi, h: (bi, 0, h)),
                  pl.BlockSpec((None, tk, g * V_DIM), lambda bi, h: (bi, 0, h))],
        out_specs=pl.BlockSpec((t, g * V_DIM), lambda bi, h: (bi, h)),
        compiler_params=_params("arbitrary", "arbitrary"),
        name="latent_attn_sample",
    )(q, k, v)


def _rope_tables(pos):
    half = ROPE_DIM // 2
    inv = ROPE_THETA ** (-jnp.arange(half, dtype=F32) / half)
    ang = pos.astype(F32)[:, None] * inv[None, :]
    cos, sin = jnp.cos(ang), jnp.sin(ang)
    return (jnp.concatenate([cos, cos, cos, cos], -1), jnp.concatenate([-sin, sin, -sin, sin], -1))


def _swap_halves(w):
    half = ROPE_DIM // 2
    return jnp.concatenate([w[..., half:], w[..., :half]], -1)


def _rope_gains(g):
    z = jnp.zeros((LANES - g.shape[0],), g.dtype)
    return jnp.concatenate([g, z])[None, :], jnp.concatenate([z, g])[None, :]


class _Stream:
    def __init__(self, x, row0, pos0):
        self.b, self.s, self.d = x.shape
        self.m = self.b * self.s
        self.row0 = row0
        self.pos0 = pos0
        self.per_row_gate = self.s < 128


def _gate_operand(st, mod, ls):
    d = st.d
    if st.per_row_gate:
        g = jnp.repeat(mod[ls, st.row0:st.row0 + st.b, 2 * d:], st.s, axis=0)
        return g, lambda tm, tn: pl.BlockSpec((tm, tn), lambda i, j, kk: (i, j))
    mod4 = mod.reshape(mod.shape[0], mod.shape[1], 1, 3 * d)

    def spec(tm, tn):
        per_b = st.s // tm
        return pl.BlockSpec((None, None, 1, tn),
                            lambda i, j, kk: (ls, st.row0 + i // per_b, 0, 2 * d // tn + j))
    return mod4, spec


def _mlp_sublayer(st, x2d, mod, norm_g4, ls, w1, w2, layer):
    d = st.d
    mod4 = mod.reshape(mod.shape[0], mod.shape[1], 1, 3 * d)
    h = _adanorm(x2d.reshape(st.b, st.s, d), norm_g4, mod4, ls, st.row0)
    dff = w1.shape[-1]
    tm = _row_tile(st.m)
    tn = min(512, dff)
    a, = _matmul(h, w1, 0, dff, tn, _epi_relu2, [],
                 [((st.m, dff), BF16, (tm, tn), lambda i, j: (i, j))], "mlp_up", layer=layer)
    gate, gate_spec = _gate_operand(st, mod, ls)
    return _resid_matmul(a, w2, x2d, gate, gate_spec, "mlp_down", layer=layer)


def _even_layer(st, x2d, mod, norm_g4, ls, w_in, g_q, g_k, bias, w_out, cache):
    d = st.d
    m = st.m
    mod4 = mod.reshape(mod.shape[0], mod.shape[1], 1, 3 * d)
    h = _adanorm(x2d.reshape(st.b, st.s, d), norm_g4, mod4, ls, st.row0)
    tm = _row_tile(m)
    wa = H_A * HEAD_DIM
    wb = H_B * HEAD_DIM
    d_ab = wa + wb
    tn = 512
    scale = HEAD_DIM ** -0.5 * LOG2E
    blk = lambda i, j: (i, j)
    bf = lambda: ((m, wa), BF16, (tm, tn), blk)
    f32 = lambda: ((m, wa), F32, (tm, tn), blk)

    def gain(g):
        return [(g[None, :], (1, HEAD_DIM), lambda i, j: (0, 0))]

    qa, = _matmul(h, w_in, 0, wa, tn, functools.partial(_epi_headnorm, scale=scale), gain(g_q),
                  [bf()], "proj_qa")
    qb, = _matmul(h, w_in, wa, wb, tn, functools.partial(_epi_plain, scale=scale), [],
                  [bf()], "proj_qb")
    ka, ka32 = _matmul(h, w_in, d_ab, wa, tn, _epi_headnorm, gain(g_k), [bf(), f32()], "proj_ka")
    kb, kb32 = _matmul(h, w_in, d_ab + wa, wb, tn, _epi_plain, [], [bf(), f32()], "proj_kb")
    va, va32 = _matmul(h, w_in, 2 * d_ab, wa, tn, _epi_plain, [], [bf(), f32()], "proj_va")
    vb, vb32 = _matmul(h, w_in, 2 * d_ab + wa, wb, tn, _epi_plain, [], [bf(), f32()], "proj_vb")

    if cache is None:
        oa = _band_attention(qa, ka, va, bias, st.b, st.s)
        t = min(MXU_DIM, st.s)
        v3 = lambda a: a.reshape(st.b, st.s, wb)
        ob = _stick_breaking(v3(qb), v3(kb), v3(vb), st.b, st.s, st.s, t, t, 0).reshape(m, wb)
    else:
        ca_k, ca_v, cb_k, cb_v = cache
        n_keep = ca_k.shape[1]
        assert st.s <= CHUNK and n_keep + st.s <= BAND_W and n_keep == BAND_PAST

        def with_cache(c, new, total):
            c = c.reshape(st.b, c.shape[1], -1).astype(BF16)
            new = new.reshape(st.b, st.s, -1)
            pad = jnp.zeros((st.b, total - c.shape[1] - st.s, c.shape[2]), BF16)
            return jnp.concatenate([c, new, pad], 1)

        oa = _band_attention_sample(qa, with_cache(ca_k, ka, BAND_W), with_cache(ca_v, va, BAND_W),
                                    bias, st.b, st.s, n_keep + st.s)
        past = cb_k.shape[1]
        tk = MXU_DIM
        assert past % tk == 0
        total = past + tk
        ob = _stick_breaking(qb.reshape(st.b, st.s, wb), with_cache(cb_k, kb, total),
                             with_cache(cb_v, vb, total), st.b, st.s, total, st.s, tk, past)
        ob = ob.reshape(m, wb)

    gate, gate_spec = _gate_operand(st, mod, ls)
    x2d = _resid_matmul([oa, ob], w_out, x2d, gate, gate_spec, "out_ab")
    r4 = lambda a, hh: a.reshape(st.b, st.s, hh, HEAD_DIM)
    return x2d, (r4(ka32, H_A), r4(va32, H_A), r4(kb32, H_B), r4(vb32, H_B))


def _odd_layer(st, x2d, mod, norm_g4, ls, wq_a, g_q_lat, wq_b, wkv_c, wkv_pe, g_kv_lat, wkv_bk, wkv_bv,
               g_q, g_k, w_out, cache):
    d = st.d
    m = st.m
    mod4 = mod.reshape(mod.shape[0], mod.shape[1], 1, 3 * d)
    h = _adanorm(x2d.reshape(st.b, st.s, d), norm_g4, mod4, ls, st.row0)
    tm = _row_tile(m)
    blk = lambda i, j: (i, j)
    row_blk = lambda i, j: (i, 0)
    one = lambda i, j: (0, 0)
    q_lora = wq_a.shape[1]
    kv_lora = wkv_c.shape[1]

    cos, sin = _rope_tables(st.pos0 + jnp.arange(st.s))
    per_b = st.s // tm if st.s >= tm else 0
    if per_b:
        tab_blk = lambda i, j: (i % per_b, 0)
    else:
        cos, sin = jnp.tile(cos, (st.b, 1)), jnp.tile(sin, (st.b, 1))
        tab_blk = row_blk
    tables = [(cos, (tm, LANES), tab_blk), (sin, (tm, LANES), tab_blk)]

    q_lat, = _matmul(h, wq_a, 0, q_lora, q_lora, _epi_rownorm,
                     [(g_q_lat[None, :], (1, q_lora), one)],
                     [((m, q_lora), BF16, (tm, q_lora), blk)], "proj_q_lat")
    c_kv, c_kv32 = _matmul(h, wkv_c, 0, kv_lora, kv_lora, _epi_rownorm,
                           [(g_kv_lat[None, :], (1, kv_lora), one)],
                           [((m, kv_lora), BF16, (tm, kv_lora), blk),
                            ((m, kv_lora), F32, (tm, kv_lora), blk)], "proj_c_kv")
    k_pe, = _matmul(h, wkv_pe, 0, 2 * LANES, 2 * LANES, _epi_rope_shared, tables,
                    [((m, 2 * LANES), F32, (tm, 2 * LANES), blk)], "proj_k_pe")

    wide = H_C * PADDED_HEAD
    gq_n, gq_r = g_q[None, :NOPE_DIM], _rope_gains(g_q[NOPE_DIM:])
    gk_n, gk_r = g_k[None, :NOPE_DIM], _rope_gains(g_k[NOPE_DIM:])
    lane_vec = lambda g: (g, (1, LANES), one)
    q, = _matmul(q_lat, wq_b, 0, wide, 512,
                 functools.partial(_epi_q_latent, scale=QK_DIM ** -0.5 * LOG2E),
                 tables + [lane_vec(gq_n), lane_vec(gq_r[0]), lane_vec(gq_r[1])],
                 [((m, wide), BF16, (tm, 512), blk)], "proj_q")

    if cache is None:
        c_all, pe_all, mk = c_kv, k_pe, m
    else:
        cc_kv, cc_pe = cache
        past = cc_kv.shape[1]
        total = past + MXU_DIM
        pad = total - past - st.s
        c_all = jnp.concatenate([cc_kv.astype(BF16), c_kv.reshape(st.b, st.s, kv_lora),
                                 jnp.zeros((st.b, pad, kv_lora), BF16)], 1)
        gap = jnp.zeros(cc_pe.shape[:2] + (2 * (LANES - ROPE_DIM),), F32)
        pe_c = jnp.concatenate([cc_pe, gap, cc_pe], -1)
        pe_all = jnp.concatenate([pe_c, k_pe.reshape(st.b, st.s, 2 * LANES),
                                  jnp.zeros((st.b, pad, 2 * LANES), F32)], 1)
        mk = st.b * total
        c_all, pe_all = c_all.reshape(mk, kv_lora), pe_all.reshape(mk, 2 * LANES)

    tmk = _row_tile(mk)
    nk = H_C * NOPE_DIM
    k, = _matmul(c_all, wkv_bk, 0, nk, 512, _epi_k_latent,
                 [(pe_all, (tmk, 2 * LANES), row_blk), lane_vec(gk_n), lane_vec(gk_r[0]),
                  lane_vec(gk_r[1])],
                 [((mk, wide), BF16, (tmk, 2 * 512), blk)], "proj_k")
    v, = _matmul(c_all, wkv_bv, 0, H_C * V_DIM, 512, _epi_plain, [],
                 [((mk, H_C * V_DIM), BF16, (tmk, 512), blk)], "proj_v")

    if cache is None:
        o = _latent_attention(q, k, v, st.b, st.s)
    else:
        o = _latent_attention_sample(q, k.reshape(st.b, total, wide),
                                     v.reshape(st.b, total, H_C * V_DIM),
                                     st.b, st.s, total, past + st.s, st.pos0)
    gate, gate_spec = _gate_operand(st, mod, ls)
    x2d = _resid_matmul(o, w_out, x2d, gate, gate_spec, "out_c")
    return x2d, (c_kv32.reshape(st.b, st.s, kv_lora), k_pe[:, :ROPE_DIM].reshape(st.b, st.s, ROPE_DIM))


def kernel(x_prompt, x_sample, cache_a_k, cache_a_v, cache_b_k, cache_b_v, cache_c_kv, cache_c_pe,
           c_prompt, c_sample, norm_g, ada_w, ada_b, w_in_ab, g_q_a, g_k_a, rel_bias_a, w_out_ab,
           wq_a_c, g_q_lat_c, wq_b_c, wkv_a_c, g_kv_lat_c, wkv_b_c, g_q_c, g_k_c, w_out_c,
           mlp_w1, mlp_w2):
    bp, s, d = x_prompt.shape
    bs, t, _ = x_sample.shape
    depth = norm_g.shape[0]
    past_len = cache_b_k.shape[2]
    n_keep = cache_a_k.shape[2]
    assert s >= n_keep

    n_rows = -(-(bp + bs) // 16) * 16
    c_rows = jnp.concatenate([c_prompt, c_sample, jnp.zeros((n_rows - bp - bs, d), F32)], 0)
    mod = _ada_params(c_rows, ada_w.reshape(depth * 2, d, 3 * d), ada_b.reshape(depth * 2, 1, 3 * d))
    norm_g4 = norm_g.reshape(depth * 2, 1, d)
    w1_all, w2_all = mlp_w1.astype(BF16), mlp_w2.astype(BF16)

    prompt = _Stream(x_prompt, 0, 0)
    sample = _Stream(x_sample, bp, past_len)
    xp = x_prompt.reshape(bp * s, d)
    xs = x_sample.reshape(bs * t, d)

    outs_p = {"a": [], "c": []}
    outs_s = {"a": [], "c": []}
    for layer in range(depth):
        ls = 2 * layer
        if layer % 2 == 0:
            e = layer // 2
            w_in = w_in_ab[e].astype(BF16)
            w_out = w_out_ab[e].astype(BF16)
            bias = _band_bias(rel_bias_a[e])
            xp, keep = _even_layer(prompt, xp, mod, norm_g4, ls, w_in, g_q_a[e], g_k_a[e], bias,
                                   w_out, None)
            outs_p["a"].append(keep)
            xs, keep = _even_layer(sample, xs, mod, norm_g4, ls, w_in, g_q_a[e], g_k_a[e], bias,
                                   w_out, (cache_a_k[e], cache_a_v[e], cache_b_k[e], cache_b_v[e]))
            outs_s["a"].append(keep)
        else:
            mi = layer // 2
            kv_lora = g_kv_lat_c.shape[1]
            wq_a = wq_a_c[mi].astype(BF16)
            wq_b = wq_b_c[mi].astype(BF16).reshape(-1, H_C // 2, 2, QK_DIM)
            q_lora = wq_b.shape[0]
            wq_n = wq_b[..., :NOPE_DIM].reshape(q_lora, H_C // 2, 2 * NOPE_DIM)
            wq_r = wq_b[..., NOPE_DIM:]
            wq_b = jnp.concatenate([wq_n, wq_r.reshape(q_lora, H_C // 2, 2 * ROPE_DIM),
                                    _swap_halves(wq_r).reshape(q_lora, H_C // 2, 2 * ROPE_DIM)], -1)
            wq_b = wq_b.reshape(q_lora, H_C * PADDED_HEAD)
            wkv_a = wkv_a_c[mi]
            wkv_c = wkv_a[:, :kv_lora].astype(BF16)
            wkv_pe = wkv_a[:, kv_lora:].astype(BF16)
            wkv_sw = _swap_halves(wkv_pe)
            wkv_pe = jnp.concatenate([wkv_pe, wkv_sw, wkv_sw, wkv_pe], -1)
            wkv_b = wkv_b_c[mi].reshape(kv_lora, H_C, NOPE_DIM + V_DIM)
            wkv_bk = wkv_b[..., :NOPE_DIM].reshape(kv_lora, H_C * NOPE_DIM).astype(BF16)
            wkv_bv = wkv_b[..., NOPE_DIM:].reshape(kv_lora, H_C * V_DIM).astype(BF16)
            w_out = w_out_c[mi].astype(BF16)
            args = (wq_a, g_q_lat_c[mi], wq_b, wkv_c, wkv_pe, g_kv_lat_c[mi], wkv_bk, wkv_bv,
                    g_q_c[mi], g_k_c[mi], w_out)
            xp, keep = _odd_layer(prompt, xp, mod, norm_g4, ls, *args, None)
            outs_p["c"].append(keep)
            xs, keep = _odd_layer(sample, xs, mod, norm_g4, ls, *args,
                                  (cache_c_kv[mi], cache_c_pe[mi]))
            outs_s["c"].append(keep)
        xp = _mlp_sublayer(prompt, xp, mod, norm_g4, ls + 1, w1_all, w2_all, layer)
        xs = _mlp_sublayer(sample, xs, mod, norm_g4, ls + 1, w1_all, w2_all, layer)

    stack = lambda items, i: jnp.stack([it[i] for it in items])
    tail = lambda a: a[:, :, s - n_keep:]
    return (xp.reshape(bp, s, d), xs.reshape(bs, t, d),
            tail(stack(outs_p["a"], 0)), tail(stack(outs_p["a"], 1)),
            stack(outs_s["a"], 0), stack(outs_s["a"], 1),
            stack(outs_p["a"], 2), stack(outs_p["a"], 3),
            stack(outs_s["a"], 2), stack(outs_s["a"], 3),
            stack(outs_p["c"], 0), stack(outs_p["c"], 1),
            stack(outs_s["c"], 0), stack(outs_s["c"], 1))
```

```python
import functools
import math

import jax
import jax.numpy as jnp
from jax import lax
from jax.experimental import pallas as pl
from jax.experimental.pallas import tpu as pltpu

CHUNK = 64
BAND_CHUNKS = 8
BAND_PAST = BAND_CHUNKS * CHUNK
HEAD_DIM = 128
H_A = 16
H_B = 16
REL_CLIP = 128
N_REL = 2 * REL_CLIP + 1
H_C = 32
NOPE_DIM = 128
ROPE_DIM = 64
QK_DIM = NOPE_DIM + ROPE_DIM
V_DIM = 128
ROPE_THETA = 10000.0
EPS = 1e-6

LANES = 128
MXU_DIM = 256
VMEM_LIMIT_BYTES = 52 * 1024 * 1024

SB_EXIT = 152.0
SB_GROUP = 4

BAND_Q = 2 * CHUNK
BAND_W = BAND_PAST + BAND_Q
BAND_UNROLL = 10
PADDED_HEAD = 2 * LANES

LOG2E = math.log2(math.e)
F32 = jnp.float32
BF16 = jnp.bfloat16
_NT = (((1,), (1,)), ((), ()))


def _chunk_of(pos):
    return jnp.right_shift(pos, int(math.log2(CHUNK)))


def _params(*sem):
    return pltpu.CompilerParams(dimension_semantics=sem, vmem_limit_bytes=VMEM_LIMIT_BYTES)


def _row_tile(m, cap=1024):
    t = min(m, cap)
    while m % t:
        t //= 2
    return t


def _ada_kernel(c_ref, w_ref, b_ref, o_ref, *, d_tiles):
    j = pl.program_id(1)
    c = c_ref[...]
    a = (c * jax.nn.sigmoid(c)).astype(BF16)
    acc = jnp.dot(a, w_ref[...].astype(BF16), preferred_element_type=F32) + b_ref[...]
    is_scale = jnp.logical_and(j >= d_tiles, j < 2 * d_tiles)
    o_ref[...] = acc + jnp.where(is_scale, 1.0, 0.0).astype(F32)


def _ada_params(c_rows, ada_w, ada_b):
    n_l, d, d3 = ada_w.shape
    r = c_rows.shape[0]
    tn = min(512, d)
    return pl.pallas_call(
        functools.partial(_ada_kernel, d_tiles=d // tn),
        out_shape=jax.ShapeDtypeStruct((n_l, r, d3), F32),
        grid=(n_l, d3 // tn),
        in_specs=[pl.BlockSpec((r, d), lambda l, j: (0, 0)),
                  pl.BlockSpec((None, d, tn), lambda l, j: (l, 0, j)),
                  pl.BlockSpec((None, 1, tn), lambda l, j: (l, 0, j))],
        out_specs=pl.BlockSpec((None, r, tn), lambda l, j: (l, 0, j)),
        compiler_params=_params("arbitrary", "arbitrary"),
        name="ada_params",
    )(c_rows, ada_w, ada_b)


def _adanorm_kernel(x_ref, g_ref, sh_ref, sc_ref, o_ref):
    x = x_ref[...]
    ms = jnp.mean(x * x, axis=-1, keepdims=True)
    y = x * lax.rsqrt(ms + EPS) * g_ref[...]
    o_ref[...] = (y * sc_ref[...] + sh_ref[...]).astype(o_ref.dtype)


def _adanorm(x, g, mod4, ls, row0):
    b, s, d = x.shape
    tm = _row_tile(s, 512)
    nt = s // tm
    return pl.pallas_call(
        _adanorm_kernel,
        out_shape=jax.ShapeDtypeStruct((b * s, d), BF16),
        grid=(b, nt),
        in_specs=[pl.BlockSpec((None, tm, d), lambda bi, i: (bi, i, 0)),
                  pl.BlockSpec((None, 1, d), lambda bi, i: (ls, 0, 0)),
                  pl.BlockSpec((None, None, 1, d), lambda bi, i: (ls, row0 + bi, 0, 0)),
                  pl.BlockSpec((None, None, 1, d), lambda bi, i: (ls, row0 + bi, 0, 1))],
        out_specs=pl.BlockSpec((tm, d), lambda bi, i: (bi * nt + i, 0)),
        compiler_params=_params("arbitrary", "arbitrary"),
        name="adanorm",
    )(x, g, mod4, mod4)


def _mm_kernel(*refs, n_extra, epilogue, n_split):
    x_ref, w_ref = refs[:2]
    extra = refs[2:2 + n_extra]
    outs = refs[2 + n_extra:]
    if n_split == 1:
        epilogue(jnp.dot(x_ref[...], w_ref[...], preferred_element_type=F32), extra, outs)
        return
    sub = w_ref.shape[1] // n_split
    for c in range(n_split):
        acc = jnp.dot(x_ref[...], w_ref[:, c * sub:(c + 1) * sub], preferred_element_type=F32)
        views = []
        for o in outs:
            width = o.shape[1] // n_split
            views.append(o.at[:, pl.ds(c * width, width)])
        epilogue(acc, extra, views)


def _matmul(x, w, col0, ncols, tn, epilogue, extras, outs, name, layer=0, n_split=1):
    m, k = x.shape
    tm = _row_tile(m)
    assert ncols % tn == 0 and col0 % tn == 0
    j0 = col0 // tn
    if w.ndim == 3:
        w_spec = pl.BlockSpec((None, k, tn), lambda i, j: (layer, 0, j0 + j))
    else:
        w_spec = pl.BlockSpec((k, tn), lambda i, j: (0, j0 + j))
    in_specs = [pl.BlockSpec((tm, k), lambda i, j: (i, 0)), w_spec]
    in_specs += [pl.BlockSpec(bs, im) for _, bs, im in extras]
    return pl.pallas_call(
        functools.partial(_mm_kernel, n_extra=len(extras), epilogue=epilogue, n_split=n_split),
        out_shape=[jax.ShapeDtypeStruct(sh, dt) for sh, dt, _, _ in outs],
        grid=(m // tm, ncols // tn),
        in_specs=in_specs,
        out_specs=[pl.BlockSpec(bs, im) for _, _, bs, im in outs],
        compiler_params=_params("arbitrary", "arbitrary"),
        name=name,
    )(x, w, *[a for a, _, _ in extras])


def _epi_plain(acc, extra, outs, *, scale=1.0):
    for o in outs:
        o[...] = (acc * scale if scale != 1.0 else acc).astype(o.dtype)


def _epi_relu2(acc, extra, outs):
    outs[0][...] = jnp.square(jnp.maximum(acc, 0.0)).astype(outs[0].dtype)


def _epi_headnorm(acc, extra, outs, *, scale=1.0):
    g = extra[0][...]
    for h in range(acc.shape[1] // HEAD_DIM):
        sl = slice(h * HEAD_DIM, (h + 1) * HEAD_DIM)
        a = acc[:, sl]
        y = a * lax.rsqrt(jnp.mean(a * a, axis=-1, keepdims=True) + EPS) * g
        if scale != 1.0:
            y = y * scale
        for o in outs:
            o[:, sl] = y.astype(o.dtype)


def _epi_rownorm(acc, extra, outs):
    g = extra[0][...]
    y = acc * lax.rsqrt(jnp.mean(acc * acc, axis=-1, keepdims=True) + EPS) * g
    for o in outs:
        o[...] = y.astype(o.dtype)


def _epi_rope_shared(acc, extra, outs):
    cos, sin = extra[0][...], extra[1][...]
    a, b = acc[:, :LANES], acc[:, LANES:]
    lo = lax.broadcasted_iota(jnp.int32, a.shape, 1) < ROPE_DIM
    outs[0][:, :LANES] = jnp.where(lo, a * cos + b * sin, 0.0)
    outs[0][:, LANES:] = jnp.where(lo, 0.0, b * cos + a * sin)


def _epi_q_latent(acc, extra, outs, *, scale):
    cos, sin = extra[0][...], extra[1][...]
    gn, gr = extra[2][...], (extra[3][...], extra[4][...])
    pair = 2 * PADDED_HEAD
    lo = lax.broadcasted_iota(jnp.int32, cos.shape, 1) < ROPE_DIM
    for p in range(acc.shape[1] // pair):
        c0 = p * pair
        r = (acc[:, c0 + 2 * NOPE_DIM:c0 + 3 * NOPE_DIM] * cos
             + acc[:, c0 + 3 * NOPE_DIM:c0 + 4 * NOPE_DIM] * sin)
        r2 = r * r
        for h in range(2):
            n = acc[:, c0 + h * NOPE_DIM:c0 + (h + 1) * NOPE_DIM]
            mine = lo if h == 0 else jnp.logical_not(lo)
            ss = jnp.sum(n * n + jnp.where(mine, r2, 0.0), axis=-1, keepdims=True)
            inv = lax.rsqrt(ss * (1.0 / QK_DIM) + EPS) * scale
            o0 = c0 + h * PADDED_HEAD
            outs[0][:, o0:o0 + NOPE_DIM] = (n * inv * gn).astype(BF16)
            outs[0][:, o0 + NOPE_DIM:o0 + PADDED_HEAD] = (r * inv * gr[h]).astype(BF16)


def _epi_k_latent(acc, extra, outs):
    pe = (extra[0][:, :LANES], extra[0][:, LANES:])
    gn, gr = extra[1][...], (extra[2][...], extra[3][...])
    pe2 = pe[0] * pe[0]
    for h in range(acc.shape[1] // NOPE_DIM):
        n = acc[:, h * NOPE_DIM:(h + 1) * NOPE_DIM]
        ss = jnp.sum(n * n + pe2, axis=-1, keepdims=True)
        inv = lax.rsqrt(ss * (1.0 / QK_DIM) + EPS)
        c0 = h * PADDED_HEAD
        outs[0][:, c0:c0 + NOPE_DIM] = (n * inv * gn).astype(BF16)
        outs[0][:, c0 + NOPE_DIM:c0 + PADDED_HEAD] = (pe[h % 2] * inv * gr[h % 2]).astype(BF16)


def _resid_kernel(*refs, nk, n_a):
    a_refs = refs[:n_a]
    w_ref, x_ref, g_ref, o_ref = refs[n_a:n_a + 4]
    if nk == 1:
        acc = jnp.dot(a_refs[0][...], w_ref[...], preferred_element_type=F32)
        o_ref[...] = x_ref[...] + g_ref[...] * acc
        return
    acc_ref, = refs[n_a + 4:]
    k = pl.program_id(2)

    @pl.when(k == 0)
    def _():
        acc_ref[...] = jnp.zeros_like(acc_ref)

    if n_a == 1:
        acc_ref[...] += jnp.dot(a_refs[0][...], w_ref[...], preferred_element_type=F32)
    else:
        for idx, a_ref in enumerate(a_refs):
            @pl.when(k == idx)
            def _(a_ref=a_ref):
                acc_ref[...] += jnp.dot(a_ref[...], w_ref[...], preferred_element_type=F32)

    @pl.when(k == nk - 1)
    def _():
        o_ref[...] = x_ref[...] + g_ref[...] * acc_ref[...]


def _resid_matmul(a, w, x, gate, gate_spec, name, layer=0):
    a_list = list(a) if isinstance(a, (list, tuple)) else [a]
    m = a_list[0].shape[0]
    d = w.shape[-1]
    tm = _row_tile(m)
    tn = min(1024, d)
    if len(a_list) == 1:
        k = a_list[0].shape[1]
        tk = min(2048, k)
        nk = k // tk
        a_specs = [pl.BlockSpec((tm, tk), lambda i, j, kk: (i, kk))]
    else:
        tk = a_list[0].shape[1]
        nk = len(a_list)
        assert all(t.shape == (m, tk) for t in a_list)
        a_specs = [pl.BlockSpec((tm, tk), lambda i, j, kk: (i, 0)) for _ in a_list]
    scratch = [] if nk == 1 else [pltpu.VMEM((tm, tn), F32)]
    if w.ndim == 3:
        w_spec = pl.BlockSpec((None, tk, tn), lambda i, j, kk: (layer, kk, j))
    else:
        w_spec = pl.BlockSpec((tk, tn), lambda i, j, kk: (kk, j))
    return pl.pallas_call(
        functools.partial(_resid_kernel, nk=nk, n_a=len(a_list)),
        out_shape=jax.ShapeDtypeStruct((m, d), F32),
        grid=(m // tm, d // tn, nk),
        in_specs=a_specs + [w_spec,
                            pl.BlockSpec((tm, tn), lambda i, j, kk: (i, j)),
                            gate_spec(tm, tn)],
        out_specs=pl.BlockSpec((tm, tn), lambda i, j, kk: (i, j)),
        scratch_shapes=scratch,
        compiler_params=_params("arbitrary", "arbitrary", "arbitrary"),
        name=name,
    )(*a_list, w, x, gate)


def _bias_kernel(tab_ref, o_ref):
    h = pl.program_id(0)
    row = lax.broadcasted_iota(jnp.int32, (BAND_Q, BAND_W), 0)
    col = lax.broadcasted_iota(jnp.int32, (BAND_Q, BAND_W), 1)
    idx = jnp.clip(BAND_PAST + row - col, -REL_CLIP, REL_CLIP) + REL_CLIP
    rel = BAND_CHUNKS + _chunk_of(row) - _chunk_of(col)
    valid = jnp.logical_and(rel >= 0, rel <= BAND_CHUNKS)

    def body(t, acc):
        return jnp.where(idx == t, tab_ref[h, t], acc)

    acc = lax.fori_loop(0, N_REL, body, jnp.zeros((BAND_Q, BAND_W), F32))
    o_ref[...] = jnp.where(valid, acc * LOG2E, -jnp.inf)


def _band_bias(rel_bias):
    return pl.pallas_call(
        _bias_kernel,
        out_shape=jax.ShapeDtypeStruct((H_A, BAND_Q, BAND_W), F32),
        grid=(H_A,),
        in_specs=[pl.BlockSpec(memory_space=pltpu.SMEM)],
        out_specs=pl.BlockSpec((None, BAND_Q, BAND_W), lambda h: (h, 0, 0)),
        compiler_params=_params("arbitrary"),
        name="band_bias",
    )(rel_bias)


def _softmax_attend(q, k, v, bias):
    s = lax.dot_general(q, k, _NT, preferred_element_type=F32) + bias
    m = jnp.max(s, axis=-1, keepdims=True)
    p = jnp.exp2(s - m)
    l = jnp.sum(p, axis=-1, keepdims=True)
    o = jnp.dot(p.astype(BF16), v, preferred_element_type=F32)
    return o / l


def _band_kernel(q_ref, k_ref, v_ref, bias_ref, o_ref, *, nq):
    lead = BAND_PAST // BAND_Q
    for i in range(min(lead, nq)):
        rows = slice(i * BAND_Q, (i + 1) * BAND_Q)
        keys = slice(0, (i + 1) * BAND_Q)
        bias = bias_ref[:, (lead - i) * BAND_Q:]
        o_ref[rows, :] = _softmax_attend(q_ref[rows, :], k_ref[keys, :], v_ref[keys, :],
                                         bias).astype(o_ref.dtype)

    def body(i, c):
        r0 = pl.multiple_of(i * BAND_Q, BAND_Q)
        k0 = pl.multiple_of((i - lead) * BAND_Q, BAND_Q)
        o = _softmax_attend(q_ref[pl.ds(r0, BAND_Q), :], k_ref[pl.ds(k0, BAND_W), :],
                            v_ref[pl.ds(k0, BAND_W), :], bias_ref[...])
        o_ref[pl.ds(r0, BAND_Q), :] = o.astype(o_ref.dtype)
        return c

    if nq > lead:
        unroll = BAND_UNROLL if (nq - lead) % BAND_UNROLL == 0 else 1
        lax.fori_loop(lead, nq, body, 0, unroll=unroll)


def _band_attention(q, k, v, bias, b, s):
    assert s % BAND_Q == 0
    view = lambda t: t.reshape(b, s, H_A * HEAD_DIM)
    spec = pl.BlockSpec((None, s, HEAD_DIM), lambda bi, h: (bi, 0, h))
    out = pl.pallas_call(
        functools.partial(_band_kernel, nq=s // BAND_Q),
        out_shape=jax.ShapeDtypeStruct((b, s, H_A * HEAD_DIM), BF16),
        grid=(b, H_A),
        in_specs=[spec, spec, spec,
                  pl.BlockSpec((None, BAND_Q, BAND_W), lambda bi, h: (h, 0, 0))],
        out_specs=spec,
        compiler_params=_params("arbitrary", "arbitrary"),
        name="band_attn",
    )(view(q), view(k), view(v), bias)
    return out.reshape(b * s, H_A * HEAD_DIM)


def _band_sample_kernel(q_ref, k_ref, v_ref, bias_ref, o_ref, *, n_valid):
    t = q_ref.shape[0]
    col = lax.broadcasted_iota(jnp.int32, (t, BAND_W), 1)
    bias = jnp.where(col < n_valid, bias_ref[0:t, :], -jnp.inf)
    o_ref[...] = _softmax_attend(q_ref[...], k_ref[...], v_ref[...], bias).astype(o_ref.dtype)


def _band_attention_sample(q, kk, vv, bias, bs, t, n_valid):
    kv_spec = pl.BlockSpec((None, BAND_W, HEAD_DIM), lambda bi, h: (bi, 0, h))
    q_spec = pl.BlockSpec((t, HEAD_DIM), lambda bi, h: (bi, h))
    return pl.pallas_call(
        functools.partial(_band_sample_kernel, n_valid=n_valid),
        out_shape=jax.ShapeDtypeStruct((bs * t, H_A * HEAD_DIM), BF16),
        grid=(bs, H_A),
        in_specs=[q_spec, kv_spec, kv_spec,
                  pl.BlockSpec((None, BAND_Q, BAND_W), lambda bi, h: (h, 0, 0))],
        out_specs=q_spec,
        compiler_params=_params("arbitrary", "arbitrary"),
        name="band_attn_sample",
    )(q, kk, vv, bias)


def _sb_kernel(q_ref, k_ref, v_ref, o_ref, acc_ref, run_ref, *, nq, tq, tk, qoff, group):
    above = (lax.broadcasted_iota(jnp.int32, (tk, tk), 0)
             > lax.broadcasted_iota(jnp.int32, (tk, tk), 1)).astype(BF16)

    def block(q, j, run, causal):
        k0 = pl.multiple_of(j * tk, tk)
        z = lax.dot_general(q, k_ref[pl.ds(k0, tk), :], _NT, preferred_element_type=F32)
        log_sig = jnp.minimum(z, 0.0) - jnp.log2(1.0 + jnp.exp2(-jnp.abs(z)))
        log_keep = log_sig - z
        if causal is not None:
            log_keep = jnp.where(causal, log_keep, 0.0)
        after = jnp.dot(log_keep.astype(BF16), above, preferred_element_type=F32) + run
        w = jnp.exp2(log_sig + after)
        if causal is not None:
            w = jnp.where(causal, w, 0.0)
        o = jnp.dot(w.astype(BF16), v_ref[pl.ds(k0, tk), :], preferred_element_type=F32)
        return o, run + jnp.sum(log_keep, axis=-1, keepdims=True)

    row = lax.broadcasted_iota(jnp.int32, (tq, tk), 0)
    col = lax.broadcasted_iota(jnp.int32, (tq, tk), 1)

    def nearest_blocks(qi, u):
        r0 = pl.multiple_of(qi * tq, tq)
        q = q_ref[pl.ds(r0, tq), :]
        jd = (qoff + r0) // tk
        o, run = block(q, jd, jnp.zeros((tq, 1), F32), jd * tk + col < qoff + r0 + row)
        o_prev, run = block(q, jnp.maximum(jd - 1, 0), run, col < jnp.where(jd > 0, tk, 0))
        acc_ref[u] = o + o_prev
        run_ref[u] = run
        return jd, jnp.max(run)

    def remaining_blocks(qi, u, jd, run_max):
        r0 = pl.multiple_of(qi * tq, tq)

        def cond(st):
            j, run_max = st
            return jnp.logical_and(j >= 0, run_max > -SB_EXIT)

        def body(st):
            j, _ = st
            o, run = block(q_ref[pl.ds(r0, tq), :], j, run_ref[u], None)
            acc_ref[u] += o
            run_ref[u] = run
            return j - 1, jnp.max(run)

        lax.while_loop(cond, body, (jd - 2, run_max))
        o_ref[pl.ds(r0, tq), :] = acc_ref[u].astype(o_ref.dtype)

    def q_group(gi, c):
        state = [nearest_blocks(gi * group + u, u) for u in range(group)]
        for u, (jd, run_max) in enumerate(state):
            remaining_blocks(gi * group + u, u, jd, run_max)
        return c

    lax.fori_loop(0, nq // group, q_group, 0)


def _stick_breaking(q, k, v, b, tq_total, tk_total, tq, tk, qoff):
    assert tq_total % tq == 0 and tk_total % tk == 0 and tq <= tk and tk % tq == 0 and qoff % tk == 0
    assert qoff + tq_total <= tk_total
    w = H_B * HEAD_DIM
    q_spec = pl.BlockSpec((None, tq_total, HEAD_DIM), lambda bi, h: (bi, 0, h))
    kv_spec = pl.BlockSpec((None, tk_total, HEAD_DIM), lambda bi, h: (bi, 0, h))
    nq = tq_total // tq
    group = SB_GROUP if nq % SB_GROUP == 0 else 1
    return pl.pallas_call(
        functools.partial(_sb_kernel, nq=nq, tq=tq, tk=tk, qoff=qoff, group=group),
        out_shape=jax.ShapeDtypeStruct((b, tq_total, w), BF16),
        grid=(b, H_B),
        in_specs=[q_spec, kv_spec, kv_spec],
        out_specs=q_spec,
        scratch_shapes=[pltpu.VMEM((group, tq, HEAD_DIM), F32), pltpu.VMEM((group, tq, 1), F32)],
        compiler_params=_params("arbitrary", "arbitrary"),
        name="stick_breaking",
    )(q, k, v)


def _online_softmax_step(q, k, v, carry, valid):
    m, l, acc = carry
    s = lax.dot_general(k, q, _NT, preferred_element_type=F32)
    if valid is not None:
        s = jnp.where(valid, s, -jnp.inf)
    m_new = jnp.maximum(m, jnp.max(s, axis=0, keepdims=True))
    alpha = jnp.exp2(m - m_new)
    p = jnp.exp2(s - m_new)
    l = alpha * l + jnp.sum(p, axis=0, keepdims=True)
    acc = alpha * acc + lax.dot_general(v, p.astype(BF16), (((0,), (0,)), ((), ())),
                                        preferred_element_type=F32)
    return m_new, l, acc


def _mla_kernel(q_ref, k_ref, v_ref, o_ref, *, nq, tq, tk, td):
    key = lax.broadcasted_iota(jnp.int32, (td, td), 0)
    qry = lax.broadcasted_iota(jnp.int32, (td, td), 1)
    diag_valid = _chunk_of(key) <= _chunk_of(qry)

    def q_tile(qi, c):
        r0 = pl.multiple_of(qi * tq, tq)
        q = q_ref[pl.ds(r0, tq), :]

        def kv(j, carry):
            k0 = pl.multiple_of(j * tk, tk)
            return _online_softmax_step(q, k_ref[pl.ds(k0, tk), :], v_ref[pl.ds(k0, tk), :],
                                        carry, None)

        m, l, acc = lax.fori_loop(
            0, qi * (tq // tk), kv,
            (jnp.full((1, tq), -jnp.inf, F32), jnp.zeros((1, tq), F32), jnp.zeros((V_DIM, tq), F32)))
        for e in range(tq // td):
            rows = slice(e * td, (e + 1) * td)
            part = (m[:, rows], l[:, rows], acc[:, rows])
            qe = q[rows]
            if e > 0:
                k0 = pl.multiple_of(r0, td)
                part = _online_softmax_step(qe, k_ref[pl.ds(k0, e * td), :],
                                            v_ref[pl.ds(k0, e * td), :], part, None)
            k0 = pl.multiple_of(r0 + e * td, td)
            _, le, acce = _online_softmax_step(qe, k_ref[pl.ds(k0, td), :], v_ref[pl.ds(k0, td), :],
                                               part, diag_valid)
            o_ref[pl.ds(pl.multiple_of(r0 + e * td, td), td), :] = (acce / le).T.astype(o_ref.dtype)
        return c

    lax.fori_loop(0, nq, q_tile, 0)


def _latent_attention(q, k, v, b, s, tq=2048, tk=1024, td=1024):
    tq = min(tq, s)
    tk, td = min(tk, tq), min(td, tq)
    assert s % tq == 0 and tq % tk == 0 and tq % td == 0 and td % CHUNK == 0
    qk_spec = pl.BlockSpec((None, s, PADDED_HEAD), lambda bi, h: (bi, 0, h))
    v_spec = pl.BlockSpec((None, s, V_DIM), lambda bi, h: (bi, 0, h))
    out = pl.pallas_call(
        functools.partial(_mla_kernel, nq=s // tq, tq=tq, tk=tk, td=td),
        out_shape=jax.ShapeDtypeStruct((b, s, H_C * V_DIM), BF16),
        grid=(b, H_C),
        in_specs=[qk_spec, qk_spec, v_spec],
        out_specs=v_spec,
        compiler_params=_params("arbitrary", "arbitrary"),
        name="latent_attn",
    )(q.reshape(b, s, H_C * PADDED_HEAD), k.reshape(b, s, H_C * PADDED_HEAD),
      v.reshape(b, s, H_C * V_DIM))
    return out.reshape(b * s, H_C * V_DIM)


def _mla_sample_kernel(q_ref, k_ref, v_ref, o_ref, *, n_valid, qpos0):
    t, tk = q_ref.shape[0], k_ref.shape[0]
    row = lax.broadcasted_iota(jnp.int32, (t, tk), 0)
    col = lax.broadcasted_iota(jnp.int32, (t, tk), 1)
    valid = jnp.logical_and(col < n_valid, _chunk_of(col) <= _chunk_of(qpos0 + row))
    bias = jnp.where(valid, 0.0, -jnp.inf).astype(F32)
    for g in range(q_ref.shape[1] // PADDED_HEAD):
        qk = slice(g * PADDED_HEAD, (g + 1) * PADDED_HEAD)
        vo = slice(g * V_DIM, (g + 1) * V_DIM)
        o_ref[:, vo] = _softmax_attend(q_ref[:, qk], k_ref[:, qk], v_ref[:, vo],
                                       bias).astype(o_ref.dtype)


def _latent_attention_sample(q, k, v, bs, t, tk, n_valid, qpos0, heads_per_step=4):
    g = heads_per_step
    return pl.pallas_call(
        functools.partial(_mla_sample_kernel, n_valid=n_valid, qpos0=qpos0),
        out_shape=jax.ShapeDtypeStruct((bs * t, H_C * V_DIM), BF16),
        grid=(bs, H_C // g),
        in_specs=[pl.BlockSpec((t, g * PADDED_HEAD), lambda bi, h: (bi, h)),
                  pl.BlockSpec((None, tk, g * PADDED_HEAD), lambda bi, h: (bi, 0, h)),
                  pl.BlockSpec((None, tk, g * V_DIM), lambda bi, h: (bi, 0, h))],
        out_specs=pl.BlockSpec((t, g * V_DIM), lambda bi, h: (bi, h)),
        compiler_params=_params("arbitrary", "arbitrary"),
        name="latent_attn_sample",
    )(q, k, v)


def _rope_tables(pos):
    half = ROPE_DIM // 2
    inv = ROPE_THETA ** (-jnp.arange(half, dtype=F32) / half)
    ang = pos.astype(F32)[:, None] * inv[None, :]
    cos, sin = jnp.cos(ang), jnp.sin(ang)
    return (jnp.concatenate([cos, cos, cos, cos], -1), jnp.concatenate([-sin, sin, -sin, sin], -1))


def _swap_halves(w):
    half = ROPE_DIM // 2
    return jnp.concatenate([w[..., half:], w[..., :half]], -1)


def _rope_gains(g):
    z = jnp.zeros((LANES - g.shape[0],), g.dtype)
    return jnp.concatenate([g, z])[None, :], jnp.concatenate([z, g])[None, :]


class _Stream:
    def __init__(self, x, row0, pos0):
        self.b, self.s, self.d = x.shape
        self.m = self.b * self.s
        self.row0 = row0
        self.pos0 = pos0
        self.per_row_gate = self.s < 128


def _gate_operand(st, mod, ls):
    d = st.d
    if st.per_row_gate:
        g = jnp.repeat(mod[ls, st.row0:st.row0 + st.b, 2 * d:], st.s, axis=0)
        return g, lambda tm, tn: pl.BlockSpec((tm, tn), lambda i, j, kk: (i, j))
    mod4 = mod.reshape(mod.shape[0], mod.shape[1], 1, 3 * d)

    def spec(tm, tn):
        per_b = st.s // tm
        return pl.BlockSpec((None, None, 1, tn),
                            lambda i, j, kk: (ls, st.row0 + i // per_b, 0, 2 * d // tn + j))
    return mod4, spec


def _mlp_sublayer(st, x2d, mod, norm_g4, ls, w1, w2, layer):
    d = st.d
    mod4 = mod.reshape(mod.shape[0], mod.shape[1], 1, 3 * d)
    h = _adanorm(x2d.reshape(st.b, st.s, d), norm_g4, mod4, ls, st.row0)
    dff = w1.shape[-1]
    tm = _row_tile(st.m)
    tn = min(512, dff)
    a, = _matmul(h, w1, 0, dff, tn, _epi_relu2, [],
                 [((st.m, dff), BF16, (tm, tn), lambda i, j: (i, j))], "mlp_up", layer=layer)
    gate, gate_spec = _gate_operand(st, mod, ls)
    return _resid_matmul(a, w2, x2d, gate, gate_spec, "mlp_down", layer=layer)


def _even_layer(st, x2d, mod, norm_g4, ls, w_in, g_q, g_k, bias, w_out, cache):
    d = st.d
    m = st.m
    mod4 = mod.reshape(mod.shape[0], mod.shape[1], 1, 3 * d)
    h = _adanorm(x2d.reshape(st.b, st.s, d), norm_g4, mod4, ls, st.row0)
    tm = _row_tile(m)
    wa = H_A * HEAD_DIM
    wb = H_B * HEAD_DIM
    d_ab = wa + wb
    ns = 2 if tm >= 512 else 1
    tn = 512 * ns
    scale = HEAD_DIM ** -0.5 * LOG2E
    blk = lambda i, j: (i, j)
    bf = lambda: ((m, wa), BF16, (tm, tn), blk)
    f32 = lambda: ((m, wa), F32, (tm, tn), blk)

    def gain(g):
        return [(g[None, :], (1, HEAD_DIM), lambda i, j: (0, 0))]

    def proj(col0, epi, extras, outs, name):
        return _matmul(h, w_in, col0, wa, tn, epi, extras, outs, name, n_split=ns)

    qa, = proj(0, functools.partial(_epi_headnorm, scale=scale), gain(g_q), [bf()], "proj_qa")
    qb, = proj(wa, functools.partial(_epi_plain, scale=scale), [], [bf()], "proj_qb")
    ka, ka32 = proj(d_ab, _epi_headnorm, gain(g_k), [bf(), f32()], "proj_ka")
    kb, kb32 = proj(d_ab + wa, _epi_plain, [], [bf(), f32()], "proj_kb")
    va, va32 = proj(2 * d_ab, _epi_plain, [], [bf(), f32()], "proj_va")
    vb, vb32 = proj(2 * d_ab + wa, _epi_plain, [], [bf(), f32()], "proj_vb")

    if cache is None:
        oa = _band_attention(qa, ka, va, bias, st.b, st.s)
        t = min(MXU_DIM, st.s)
        v3 = lambda a: a.reshape(st.b, st.s, wb)
        ob = _stick_breaking(v3(qb), v3(kb), v3(vb), st.b, st.s, st.s, t, t, 0).reshape(m, wb)
    else:
        ca_k, ca_v, cb_k, cb_v = cache
        n_keep = ca_k.shape[1]
        assert st.s <= CHUNK and n_keep + st.s <= BAND_W and n_keep == BAND_PAST

        def with_cache(c, new, total):
            c = c.reshape(st.b, c.shape[1], -1).astype(BF16)
            new = new.reshape(st.b, st.s, -1)
            pad = jnp.zeros((st.b, total - c.shape[1] - st.s, c.shape[2]), BF16)
            return jnp.concatenate([c, new, pad], 1)

        oa = _band_attention_sample(qa, with_cache(ca_k, ka, BAND_W), with_cache(ca_v, va, BAND_W),
                                    bias, st.b, st.s, n_keep + st.s)
        past = cb_k.shape[1]
        tk = MXU_DIM
        assert past % tk == 0
        total = past + tk
        ob = _stick_breaking(qb.reshape(st.b, st.s, wb), with_cache(cb_k, kb, total),
                             with_cache(cb_v, vb, total), st.b, st.s, total, st.s, tk, past)
        ob = ob.reshape(m, wb)

    gate, gate_spec = _gate_operand(st, mod, ls)
    x2d = _resid_matmul([oa, ob], w_out, x2d, gate, gate_spec, "out_ab")
    r4 = lambda a, hh: a.reshape(st.b, st.s, hh, HEAD_DIM)
    return x2d, (r4(ka32, H_A), r4(va32, H_A), r4(kb32, H_B), r4(vb32, H_B))


def _odd_layer(st, x2d, mod, norm_g4, ls, wq_a, g_q_lat, wq_b, wkv_c, wkv_pe, g_kv_lat, wkv_bk, wkv_bv,
               g_q, g_k, w_out, cache):
    d = st.d
    m = st.m
    mod4 = mod.reshape(mod.shape[0], mod.shape[1], 1, 3 * d)
    h = _adanorm(x2d.reshape(st.b, st.s, d), norm_g4, mod4, ls, st.row0)
    tm = _row_tile(m)
    blk = lambda i, j: (i, j)
    row_blk = lambda i, j: (i, 0)
    one = lambda i, j: (0, 0)
    q_lora = wq_a.shape[1]
    kv_lora = wkv_c.shape[1]

    cos, sin = _rope_tables(st.pos0 + jnp.arange(st.s))
    per_b = st.s // tm if st.s >= tm else 0
    if per_b:
        tab_blk = lambda i, j: (i % per_b, 0)
    else:
        cos, sin = jnp.tile(cos, (st.b, 1)), jnp.tile(sin, (st.b, 1))
        tab_blk = row_blk
    tables = [(cos, (tm, LANES), tab_blk), (sin, (tm, LANES), tab_blk)]

    q_lat, = _matmul(h, wq_a, 0, q_lora, q_lora, _epi_rownorm,
                     [(g_q_lat[None, :], (1, q_lora), one)],
                     [((m, q_lora), BF16, (tm, q_lora), blk)], "proj_q_lat")
    c_kv, c_kv32 = _matmul(h, wkv_c, 0, kv_lora, kv_lora, _epi_rownorm,
                           [(g_kv_lat[None, :], (1, kv_lora), one)],
                           [((m, kv_lora), BF16, (tm, kv_lora), blk),
                            ((m, kv_lora), F32, (tm, kv_lora), blk)], "proj_c_kv")
    k_pe, = _matmul(h, wkv_pe, 0, 2 * LANES, 2 * LANES, _epi_rope_shared, tables,
                    [((m, 2 * LANES), F32, (tm, 2 * LANES), blk)], "proj_k_pe")

    wide = H_C * PADDED_HEAD
    gq_n, gq_r = g_q[None, :NOPE_DIM], _rope_gains(g_q[NOPE_DIM:])
    gk_n, gk_r = g_k[None, :NOPE_DIM], _rope_gains(g_k[NOPE_DIM:])
    lane_vec = lambda g: (g, (1, LANES), one)
    ns = 2 if tm >= 512 else 1
    q, = _matmul(q_lat, wq_b, 0, wide, 512 * ns,
                 functools.partial(_epi_q_latent, scale=QK_DIM ** -0.5 * LOG2E),
                 tables + [lane_vec(gq_n), lane_vec(gq_r[0]), lane_vec(gq_r[1])],
                 [((m, wide), BF16, (tm, 512 * ns), blk)], "proj_q", n_split=ns)

    if cache is None:
        c_all, pe_all, mk = c_kv, k_pe, m
    else:
        cc_kv, cc_pe = cache
        past = cc_kv.shape[1]
        total = past + MXU_DIM
        pad = total - past - st.s
        c_all = jnp.concatenate([cc_kv.astype(BF16), c_kv.reshape(st.b, st.s, kv_lora),
                                 jnp.zeros((st.b, pad, kv_lora), BF16)], 1)
        gap = jnp.zeros(cc_pe.shape[:2] + (2 * (LANES - ROPE_DIM),), F32)
        pe_c = jnp.concatenate([cc_pe, gap, cc_pe], -1)
        pe_all = jnp.concatenate([pe_c, k_pe.reshape(st.b, st.s, 2 * LANES),
                                  jnp.zeros((st.b, pad, 2 * LANES), F32)], 1)
        mk = st.b * total
        c_all, pe_all = c_all.reshape(mk, kv_lora), pe_all.reshape(mk, 2 * LANES)

    tmk = _row_tile(mk)
    nk = H_C * NOPE_DIM
    nsk = 2 if tmk >= 512 else 1
    k, = _matmul(c_all, wkv_bk, 0, nk, 512 * nsk, _epi_k_latent,
                 [(pe_all, (tmk, 2 * LANES), row_blk), lane_vec(gk_n), lane_vec(gk_r[0]),
                  lane_vec(gk_r[1])],
                 [((mk, wide), BF16, (tmk, 2 * 512 * nsk), blk)], "proj_k", n_split=nsk)
    v, = _matmul(c_all, wkv_bv, 0, H_C * V_DIM, 512, _epi_plain, [],
                 [((mk, H_C * V_DIM), BF16, (tmk, 512), blk)], "proj_v")

    if cache is None:
        o = _latent_attention(q, k, v, st.b, st.s)
    else:
        o = _latent_attention_sample(q, k.reshape(st.b, total, wide),
                                     v.reshape(st.b, total, H_C * V_DIM),
                                     st.b, st.s, total, past + st.s, st.pos0)
    gate, gate_spec = _gate_operand(st, mod, ls)
    x2d = _resid_matmul(o, w_out, x2d, gate, gate_spec, "out_c")
    return x2d, (c_kv32.reshape(st.b, st.s, kv_lora), k_pe[:, :ROPE_DIM].reshape(st.b, st.s, ROPE_DIM))


def kernel(x_prompt, x_sample, cache_a_k, cache_a_v, cache_b_k, cache_b_v, cache_c_kv, cache_c_pe,
           c_prompt, c_sample, norm_g, ada_w, ada_b, w_in_ab, g_q_a, g_k_a, rel_bias_a, w_out_ab,
           wq_a_c, g_q_lat_c, wq_b_c, wkv_a_c, g_kv_lat_c, wkv_b_c, g_q_c, g_k_c, w_out_c,
           mlp_w1, mlp_w2):
    bp, s, d = x_prompt.shape
    bs, t, _ = x_sample.shape
    depth = norm_g.shape[0]
    past_len = cache_b_k.shape[2]
    n_keep = cache_a_k.shape[2]
    assert s >= n_keep

    n_rows = -(-(bp + bs) // 16) * 16
    c_rows = jnp.concatenate([c_prompt, c_sample, jnp.zeros((n_rows - bp - bs, d), F32)], 0)
    mod = _ada_params(c_rows, ada_w.reshape(depth * 2, d, 3 * d), ada_b.reshape(depth * 2, 1, 3 * d))
    norm_g4 = norm_g.reshape(depth * 2, 1, d)
    w1_all, w2_all = mlp_w1.astype(BF16), mlp_w2.astype(BF16)

    prompt = _Stream(x_prompt, 0, 0)
    sample = _Stream(x_sample, bp, past_len)
    xp = x_prompt.reshape(bp * s, d)
    xs = x_sample.reshape(bs * t, d)

    outs_p = {"a": [], "c": []}
    outs_s = {"a": [], "c": []}
    for layer in range(depth):
        ls = 2 * layer
        if layer % 2 == 0:
            e = layer // 2
            w_in = w_in_ab[e].astype(BF16)
            w_out = w_out_ab[e].astype(BF16)
            bias = _band_bias(rel_bias_a[e])
            xp, keep = _even_layer(prompt, xp, mod, norm_g4, ls, w_in, g_q_a[e], g_k_a[e], bias,
                                   w_out, None)
            outs_p["a"].append(keep)
            xs, keep = _even_layer(sample, xs, mod, norm_g4, ls, w_in, g_q_a[e], g_k_a[e], bias,
                                   w_out, (cache_a_k[e], cache_a_v[e], cache_b_k[e], cache_b_v[e]))
            outs_s["a"].append(keep)
        else:
            mi = layer // 2
            kv_lora = g_kv_lat_c.shape[1]
            wq_a = wq_a_c[mi].astype(BF16)
            wq_b = wq_b_c[mi].astype(BF16).reshape(-1, H_C // 2, 2, QK_DIM)
            q_lora = wq_b.shape[0]
            wq_n = wq_b[..., :NOPE_DIM].reshape(q_lora, H_C // 2, 2 * NOPE_DIM)
            wq_r = wq_b[..., NOPE_DIM:]
            wq_b = jnp.concatenate([wq_n, wq_r.reshape(q_lora, H_C // 2, 2 * ROPE_DIM),
                                    _swap_halves(wq_r).reshape(q_lora, H_C // 2, 2 * ROPE_DIM)], -1)
            wq_b = wq_b.reshape(q_lora, H_C * PADDED_HEAD)
            wkv_a = wkv_a_c[mi]
            wkv_c = wkv_a[:, :kv_lora].astype(BF16)
            wkv_pe = wkv_a[:, kv_lora:].astype(BF16)
            wkv_sw = _swap_halves(wkv_pe)
            wkv_pe = jnp.concatenate([wkv_pe, wkv_sw, wkv_sw, wkv_pe], -1)
            wkv_b = wkv_b_c[mi].reshape(kv_lora, H_C, NOPE_DIM + V_DIM)
            wkv_bk = wkv_b[..., :NOPE_DIM].reshape(kv_lora, H_C * NOPE_DIM).astype(BF16)
            wkv_bv = wkv_b[..., NOPE_DIM:].reshape(kv_lora, H_C * V_DIM).astype(BF16)
            w_out = w_out_c[mi].astype(BF16)
            args = (wq_a, g_q_lat_c[mi], wq_b, wkv_c, wkv_pe, g_kv_lat_c[mi], wkv_bk, wkv_bv,
                    g_q_c[mi], g_k_c[mi], w_out)
            xp, keep = _odd_layer(prompt, xp, mod, norm_g4, ls, *args, None)
            outs_p["c"].append(keep)
            xs, keep = _odd_layer(sample, xs, mod, norm_g4, ls, *args,
                                  (cache_c_kv[mi], cache_c_pe[mi]))
            outs_s["c"].append(keep)
        xp = _mlp_sublayer(prompt, xp, mod, norm_g4, ls + 1, w1_all, w2_all, layer)
        xs = _mlp_sublayer(sample, xs, mod, norm_g4, ls + 1, w1_all, w2_all, layer)

    stack = lambda items, i: jnp.stack([it[i] for it in items])
    tail = lambda a: a[:, :, s - n_keep:]
    return (xp.reshape(bp, s, d), xs.reshape(bs, t, d),
            tail(stack(outs_p["a"], 0)), tail(stack(outs_p["a"], 1)),
            stack(outs_s["a"], 0), stack(outs_s["a"], 1),
            stack(outs_p["a"], 2), stack(outs_p["a"], 3),
            stack(outs_s["a"], 2), stack(outs_s["a"], 3),
            stack(outs_p["c"], 0), stack(outs_p["c"], 1),
            stack(outs_s["c"], 0), stack(outs_s["c"], 1))
```

```python
import functools
import math

import jax
import jax.numpy as jnp
from jax import lax
from jax.experimental import pallas as pl
from jax.experimental.pallas import tpu as pltpu

CHUNK = 64
BAND_CHUNKS = 8
BAND_PAST = BAND_CHUNKS * CHUNK
HEAD_DIM = 128
H_A = 16
H_B = 16
REL_CLIP = 128
N_REL = 2 * REL_CLIP + 1
H_C = 32
NOPE_DIM = 128
ROPE_DIM = 64
QK_DIM = NOPE_DIM + ROPE_DIM
V_DIM = 128
ROPE_THETA = 10000.0
EPS = 1e-6

LANES = 128
MXU_DIM = 256
VMEM_LIMIT_BYTES = 52 * 1024 * 1024

SB_EXIT = 152.0
SB_GROUP = 4

BAND_Q = 2 * CHUNK
BAND_W = BAND_PAST + BAND_Q
BAND_UNROLL = 10
PADDED_HEAD = 2 * LANES

LOG2E = math.log2(math.e)
F32 = jnp.float32
BF16 = jnp.bfloat16
_NT = (((1,), (1,)), ((), ()))


def _chunk_of(pos):
    return jnp.right_shift(pos, int(math.log2(CHUNK)))


def _params(*sem):
    return pltpu.CompilerParams(dimension_semantics=sem, vmem_limit_bytes=VMEM_LIMIT_BYTES)


def _row_tile(m, cap=1024):
    t = min(m, cap)
    while m % t:
        t //= 2
    return t


def _ada_kernel(c_ref, w_ref, b_ref, o_ref, *, d_tiles):
    j = pl.program_id(1)
    c = c_ref[...]
    a = (c * jax.nn.sigmoid(c)).astype(BF16)
    acc = jnp.dot(a, w_ref[...].astype(BF16), preferred_element_type=F32) + b_ref[...]
    is_scale = jnp.logical_and(j >= d_tiles, j < 2 * d_tiles)
    o_ref[...] = acc + jnp.where(is_scale, 1.0, 0.0).astype(F32)


def _ada_params(c_rows, ada_w, ada_b):
    n_l, d, d3 = ada_w.shape
    r = c_rows.shape[0]
    tn = min(512, d)
    return pl.pallas_call(
        functools.partial(_ada_kernel, d_tiles=d // tn),
        out_shape=jax.ShapeDtypeStruct((n_l, r, d3), F32),
        grid=(n_l, d3 // tn),
        in_specs=[pl.BlockSpec((r, d), lambda l, j: (0, 0)),
                  pl.BlockSpec((None, d, tn), lambda l, j: (l, 0, j)),
                  pl.BlockSpec((None, 1, tn), lambda l, j: (l, 0, j))],
        out_specs=pl.BlockSpec((None, r, tn), lambda l, j: (l, 0, j)),
        compiler_params=_params("arbitrary", "arbitrary"),
        name="ada_params",
    )(c_rows, ada_w, ada_b)


def _adanorm_kernel(x_ref, g_ref, sh_ref, sc_ref, o_ref):
    x = x_ref[...]
    ms = jnp.mean(x * x, axis=-1, keepdims=True)
    y = x * lax.rsqrt(ms + EPS) * g_ref[...]
    o_ref[...] = (y * sc_ref[...] + sh_ref[...]).astype(o_ref.dtype)


def _adanorm(x, g, mod4, ls, row0):
    b, s, d = x.shape
    tm = _row_tile(s, 512)
    nt = s // tm
    return pl.pallas_call(
        _adanorm_kernel,
        out_shape=jax.ShapeDtypeStruct((b * s, d), BF16),
        grid=(b, nt),
        in_specs=[pl.BlockSpec((None, tm, d), lambda bi, i: (bi, i, 0)),
                  pl.BlockSpec((None, 1, d), lambda bi, i: (ls, 0, 0)),
                  pl.BlockSpec((None, None, 1, d), lambda bi, i: (ls, row0 + bi, 0, 0)),
                  pl.BlockSpec((None, None, 1, d), lambda bi, i: (ls, row0 + bi, 0, 1))],
        out_specs=pl.BlockSpec((tm, d), lambda bi, i: (bi * nt + i, 0)),
        compiler_params=_params("arbitrary", "arbitrary"),
        name="adanorm",
    )(x, g, mod4, mod4)


def _mm_kernel(*refs, n_extra, epilogue, n_split, emit_w):
    x_ref, w_ref = refs[:2]
    extra = refs[2:2 + n_extra]
    outs = refs[2 + n_extra:]
    if emit_w:
        w = w_ref[...].astype(BF16)
        outs[-1][...] = w
        epilogue(jnp.dot(x_ref[...], w, preferred_element_type=F32), extra, outs[:-1])
        return
    if n_split == 1:
        epilogue(jnp.dot(x_ref[...], w_ref[...], preferred_element_type=F32), extra, outs)
        return
    sub = w_ref.shape[1] // n_split
    for c in range(n_split):
        acc = jnp.dot(x_ref[...], w_ref[:, c * sub:(c + 1) * sub], preferred_element_type=F32)
        views = []
        for o in outs:
            width = o.shape[1] // n_split
            views.append(o.at[:, pl.ds(c * width, width)])
        epilogue(acc, extra, views)


def _matmul(x, w, col0, ncols, tn, epilogue, extras, outs, name, layer=0, n_split=1):
    m, k = x.shape
    tm = _row_tile(m)
    assert ncols % tn == 0 and col0 % tn == 0
    j0 = col0 // tn
    emit_w = w.dtype == F32
    if emit_w:
        assert m == tm and n_split == 1
        outs = list(outs) + [((k, ncols), BF16, (k, tn), lambda i, j: (0, j))]
    if w.ndim == 3:
        w_spec = pl.BlockSpec((None, k, tn), lambda i, j: (layer, 0, j0 + j))
    else:
        w_spec = pl.BlockSpec((k, tn), lambda i, j: (0, j0 + j))
    in_specs = [pl.BlockSpec((tm, k), lambda i, j: (i, 0)), w_spec]
    in_specs += [pl.BlockSpec(bs, im) for _, bs, im in extras]
    return pl.pallas_call(
        functools.partial(_mm_kernel, n_extra=len(extras), epilogue=epilogue, n_split=n_split,
                          emit_w=emit_w),
        out_shape=[jax.ShapeDtypeStruct(sh, dt) for sh, dt, _, _ in outs],
        grid=(m // tm, ncols // tn),
        in_specs=in_specs,
        out_specs=[pl.BlockSpec(bs, im) for _, _, bs, im in outs],
        compiler_params=_params("arbitrary", "arbitrary"),
        name=name,
    )(x, w, *[a for a, _, _ in extras])


def _epi_plain(acc, extra, outs, *, scale=1.0):
    for o in outs:
        o[...] = (acc * scale if scale != 1.0 else acc).astype(o.dtype)


def _epi_relu2(acc, extra, outs):
    outs[0][...] = jnp.square(jnp.maximum(acc, 0.0)).astype(outs[0].dtype)


def _epi_headnorm(acc, extra, outs, *, scale=1.0):
    g = extra[0][...]
    for h in range(acc.shape[1] // HEAD_DIM):
        sl = slice(h * HEAD_DIM, (h + 1) * HEAD_DIM)
        a = acc[:, sl]
        y = a * lax.rsqrt(jnp.mean(a * a, axis=-1, keepdims=True) + EPS) * g
        if scale != 1.0:
            y = y * scale
        for o in outs:
            o[:, sl] = y.astype(o.dtype)


def _epi_rownorm(acc, extra, outs):
    g = extra[0][...]
    y = acc * lax.rsqrt(jnp.mean(acc * acc, axis=-1, keepdims=True) + EPS) * g
    for o in outs:
        o[...] = y.astype(o.dtype)


def _epi_rope_shared(acc, extra, outs):
    cos, sin = extra[0][...], extra[1][...]
    a, b = acc[:, :LANES], acc[:, LANES:]
    lo = lax.broadcasted_iota(jnp.int32, a.shape, 1) < ROPE_DIM
    outs[0][:, :LANES] = jnp.where(lo, a * cos + b * sin, 0.0)
    outs[0][:, LANES:] = jnp.where(lo, 0.0, b * cos + a * sin)


def _epi_q_latent(acc, extra, outs, *, scale):
    cos, sin = extra[0][...], extra[1][...]
    gn, gr = extra[2][...], (extra[3][...], extra[4][...])
    pair = 2 * PADDED_HEAD
    lo = lax.broadcasted_iota(jnp.int32, cos.shape, 1) < ROPE_DIM
    for p in range(acc.shape[1] // pair):
        c0 = p * pair
        r = (acc[:, c0 + 2 * NOPE_DIM:c0 + 3 * NOPE_DIM] * cos
             + acc[:, c0 + 3 * NOPE_DIM:c0 + 4 * NOPE_DIM] * sin)
        r2 = r * r
        for h in range(2):
            n = acc[:, c0 + h * NOPE_DIM:c0 + (h + 1) * NOPE_DIM]
            mine = lo if h == 0 else jnp.logical_not(lo)
            ss = jnp.sum(n * n + jnp.where(mine, r2, 0.0), axis=-1, keepdims=True)
            inv = lax.rsqrt(ss * (1.0 / QK_DIM) + EPS) * scale
            o0 = c0 + h * PADDED_HEAD
            outs[0][:, o0:o0 + NOPE_DIM] = (n * inv * gn).astype(BF16)
            outs[0][:, o0 + NOPE_DIM:o0 + PADDED_HEAD] = (r * inv * gr[h]).astype(BF16)


def _epi_k_latent(acc, extra, outs):
    pe = (extra[0][:, :LANES], extra[0][:, LANES:])
    gn, gr = extra[1][...], (extra[2][...], extra[3][...])
    pe2 = pe[0] * pe[0]
    for h in range(acc.shape[1] // NOPE_DIM):
        n = acc[:, h * NOPE_DIM:(h + 1) * NOPE_DIM]
        ss = jnp.sum(n * n + pe2, axis=-1, keepdims=True)
        inv = lax.rsqrt(ss * (1.0 / QK_DIM) + EPS)
        c0 = h * PADDED_HEAD
        outs[0][:, c0:c0 + NOPE_DIM] = (n * inv * gn).astype(BF16)
        outs[0][:, c0 + NOPE_DIM:c0 + PADDED_HEAD] = (pe[h % 2] * inv * gr[h % 2]).astype(BF16)


def _resid_kernel(*refs, nk, n_a, emit_w):
    a_refs = refs[:n_a]
    w_ref, x_ref, g_ref, o_ref = refs[n_a:n_a + 4]
    rest = refs[n_a + 4:]
    w = w_ref[...]
    if emit_w:
        w = w.astype(BF16)
        rest[0][...] = w
        rest = rest[1:]
    if nk == 1:
        acc = jnp.dot(a_refs[0][...], w, preferred_element_type=F32)
        o_ref[...] = x_ref[...] + g_ref[...] * acc
        return
    acc_ref, = rest
    k = pl.program_id(2)

    @pl.when(k == 0)
    def _():
        acc_ref[...] = jnp.zeros_like(acc_ref)

    if n_a == 1:
        acc_ref[...] += jnp.dot(a_refs[0][...], w, preferred_element_type=F32)
    else:
        for idx, a_ref in enumerate(a_refs):
            @pl.when(k == idx)
            def _(a_ref=a_ref):
                acc_ref[...] += jnp.dot(a_ref[...], w, preferred_element_type=F32)

    @pl.when(k == nk - 1)
    def _():
        o_ref[...] = x_ref[...] + g_ref[...] * acc_ref[...]


def _resid_matmul(a, w, x, gate, gate_spec, name, layer=0):
    a_list = list(a) if isinstance(a, (list, tuple)) else [a]
    m = a_list[0].shape[0]
    d = w.shape[-1]
    tm = _row_tile(m)
    tn = min(1024, d)
    if len(a_list) == 1:
        k = a_list[0].shape[1]
        tk = min(2048, k)
        nk = k // tk
        a_specs = [pl.BlockSpec((tm, tk), lambda i, j, kk: (i, kk))]
    else:
        tk = a_list[0].shape[1]
        nk = len(a_list)
        assert all(t.shape == (m, tk) for t in a_list)
        a_specs = [pl.BlockSpec((tm, tk), lambda i, j, kk: (i, 0)) for _ in a_list]
    scratch = [] if nk == 1 else [pltpu.VMEM((tm, tn), F32)]
    if w.ndim == 3:
        w_spec = pl.BlockSpec((None, tk, tn), lambda i, j, kk: (layer, kk, j))
    else:
        w_spec = pl.BlockSpec((tk, tn), lambda i, j, kk: (kk, j))
    out_shape = [jax.ShapeDtypeStruct((m, d), F32)]
    out_specs = [pl.BlockSpec((tm, tn), lambda i, j, kk: (i, j))]
    emit_w = w.dtype == F32
    if emit_w:
        assert m == tm
        out_shape.append(jax.ShapeDtypeStruct((nk * tk, d), BF16))
        out_specs.append(pl.BlockSpec((tk, tn), lambda i, j, kk: (kk, j)))
    res = pl.pallas_call(
        functools.partial(_resid_kernel, nk=nk, n_a=len(a_list), emit_w=emit_w),
        out_shape=out_shape,
        grid=(m // tm, d // tn, nk),
        in_specs=a_specs + [w_spec,
                            pl.BlockSpec((tm, tn), lambda i, j, kk: (i, j)),
                            gate_spec(tm, tn)],
        out_specs=out_specs,
        scratch_shapes=scratch,
        compiler_params=_params("arbitrary", "arbitrary", "arbitrary"),
        name=name,
    )(*a_list, w, x, gate)
    return tuple(res) if emit_w else res[0]


def _bias_kernel(tab_ref, o_ref):
    h = pl.program_id(0)
    row = lax.broadcasted_iota(jnp.int32, (BAND_Q, BAND_W), 0)
    col = lax.broadcasted_iota(jnp.int32, (BAND_Q, BAND_W), 1)
    idx = jnp.clip(BAND_PAST + row - col, -REL_CLIP, REL_CLIP) + REL_CLIP
    rel = BAND_CHUNKS + _chunk_of(row) - _chunk_of(col)
    valid = jnp.logical_and(rel >= 0, rel <= BAND_CHUNKS)

    def body(t, acc):
        return jnp.where(idx == t, tab_ref[h, t], acc)

    acc = lax.fori_loop(0, N_REL, body, jnp.zeros((BAND_Q, BAND_W), F32))
    o_ref[...] = jnp.where(valid, acc * LOG2E, -jnp.inf)


def _band_bias(rel_bias):
    return pl.pallas_call(
        _bias_kernel,
        out_shape=jax.ShapeDtypeStruct((H_A, BAND_Q, BAND_W), F32),
        grid=(H_A,),
        in_specs=[pl.BlockSpec(memory_space=pltpu.SMEM)],
        out_specs=pl.BlockSpec((None, BAND_Q, BAND_W), lambda h: (h, 0, 0)),
        compiler_params=_params("arbitrary"),
        name="band_bias",
    )(rel_bias)


def _softmax_attend(q, k, v, bias):
    s = lax.dot_general(q, k, _NT, preferred_element_type=F32) + bias
    m = jnp.max(s, axis=-1, keepdims=True)
    p = jnp.exp2(s - m)
    l = jnp.sum(p, axis=-1, keepdims=True)
    o = jnp.dot(p.astype(BF16), v, preferred_element_type=F32)
    return o / l


def _band_kernel(q_ref, k_ref, v_ref, bias_ref, o_ref, *, nq):
    lead = BAND_PAST // BAND_Q
    for i in range(min(lead, nq)):
        rows = slice(i * BAND_Q, (i + 1) * BAND_Q)
        keys = slice(0, (i + 1) * BAND_Q)
        bias = bias_ref[:, (lead - i) * BAND_Q:]
        o_ref[rows, :] = _softmax_attend(q_ref[rows, :], k_ref[keys, :], v_ref[keys, :],
                                         bias).astype(o_ref.dtype)

    def body(i, c):
        r0 = pl.multiple_of(i * BAND_Q, BAND_Q)
        k0 = pl.multiple_of((i - lead) * BAND_Q, BAND_Q)
        o = _softmax_attend(q_ref[pl.ds(r0, BAND_Q), :], k_ref[pl.ds(k0, BAND_W), :],
                            v_ref[pl.ds(k0, BAND_W), :], bias_ref[...])
        o_ref[pl.ds(r0, BAND_Q), :] = o.astype(o_ref.dtype)
        return c

    if nq > lead:
        unroll = BAND_UNROLL if (nq - lead) % BAND_UNROLL == 0 else 1
        lax.fori_loop(lead, nq, body, 0, unroll=unroll)


def _band_attention(q, k, v, bias, b, s):
    assert s % BAND_Q == 0
    view = lambda t: t.reshape(b, s, H_A * HEAD_DIM)
    spec = pl.BlockSpec((None, s, HEAD_DIM), lambda bi, h: (bi, 0, h))
    out = pl.pallas_call(
        functools.partial(_band_kernel, nq=s // BAND_Q),
        out_shape=jax.ShapeDtypeStruct((b, s, H_A * HEAD_DIM), BF16),
        grid=(b, H_A),
        in_specs=[spec, spec, spec,
                  pl.BlockSpec((None, BAND_Q, BAND_W), lambda bi, h: (h, 0, 0))],
        out_specs=spec,
        compiler_params=_params("arbitrary", "arbitrary"),
        name="band_attn",
    )(view(q), view(k), view(v), bias)
    return out.reshape(b * s, H_A * HEAD_DIM)


def _band_sample_kernel(q_ref, k_ref, v_ref, bias_ref, o_ref, *, n_valid):
    t = q_ref.shape[0]
    col = lax.broadcasted_iota(jnp.int32, (t, BAND_W), 1)
    bias = jnp.where(col < n_valid, bias_ref[0:t, :], -jnp.inf)
    o_ref[...] = _softmax_attend(q_ref[...], k_ref[...], v_ref[...], bias).astype(o_ref.dtype)


def _band_attention_sample(q, kk, vv, bias, bs, t, n_valid):
    kv_spec = pl.BlockSpec((None, BAND_W, HEAD_DIM), lambda bi, h: (bi, 0, h))
    q_spec = pl.BlockSpec((t, HEAD_DIM), lambda bi, h: (bi, h))
    return pl.pallas_call(
        functools.partial(_band_sample_kernel, n_valid=n_valid),
        out_shape=jax.ShapeDtypeStruct((bs * t, H_A * HEAD_DIM), BF16),
        grid=(bs, H_A),
        in_specs=[q_spec, kv_spec, kv_spec,
                  pl.BlockSpec((None, BAND_Q, BAND_W), lambda bi, h: (h, 0, 0))],
        out_specs=q_spec,
        compiler_params=_params("arbitrary", "arbitrary"),
        name="band_attn_sample",
    )(q, kk, vv, bias)


def _sb_kernel(q_ref, k_ref, v_ref, o_ref, acc_ref, run_ref, *, nq, tq, tk, qoff, group):
    above = (lax.broadcasted_iota(jnp.int32, (tk, tk), 0)
             > lax.broadcasted_iota(jnp.int32, (tk, tk), 1)).astype(BF16)

    def block(q, j, run, causal):
        k0 = pl.multiple_of(j * tk, tk)
        z = lax.dot_general(q, k_ref[pl.ds(k0, tk), :], _NT, preferred_element_type=F32)
        log_sig = jnp.minimum(z, 0.0) - jnp.log2(1.0 + jnp.exp2(-jnp.abs(z)))
        log_keep = log_sig - z
        if causal is not None:
            log_keep = jnp.where(causal, log_keep, 0.0)
        after = jnp.dot(log_keep.astype(BF16), above, preferred_element_type=F32) + run
        w = jnp.exp2(log_sig + after)
        if causal is not None:
            w = jnp.where(causal, w, 0.0)
        o = jnp.dot(w.astype(BF16), v_ref[pl.ds(k0, tk), :], preferred_element_type=F32)
        return o, run + jnp.sum(log_keep, axis=-1, keepdims=True)

    row = lax.broadcasted_iota(jnp.int32, (tq, tk), 0)
    col = lax.broadcasted_iota(jnp.int32, (tq, tk), 1)

    def nearest_blocks(qi, u):
        r0 = pl.multiple_of(qi * tq, tq)
        q = q_ref[pl.ds(r0, tq), :]
        jd = (qoff + r0) // tk
        o, run = block(q, jd, jnp.zeros((tq, 1), F32), jd * tk + col < qoff + r0 + row)
        o_prev, run = block(q, jnp.maximum(jd - 1, 0), run, col < jnp.where(jd > 0, tk, 0))
        acc_ref[u] = o + o_prev
        run_ref[u] = run
        return jd, jnp.max(run)

    def remaining_blocks(qi, u, jd, run_max):
        r0 = pl.multiple_of(qi * tq, tq)

        def cond(st):
            j, run_max = st
            return jnp.logical_and(j >= 0, run_max > -SB_EXIT)

        def body(st):
            j, _ = st
            o, run = block(q_ref[pl.ds(r0, tq), :], j, run_ref[u], None)
            acc_ref[u] += o
            run_ref[u] = run
            return j - 1, jnp.max(run)

        lax.while_loop(cond, body, (jd - 2, run_max))
        o_ref[pl.ds(r0, tq), :] = acc_ref[u].astype(o_ref.dtype)

    def q_group(gi, c):
        state = [nearest_blocks(gi * group + u, u) for u in range(group)]
        for u, (jd, run_max) in enumerate(state):
            remaining_blocks(gi * group + u, u, jd, run_max)
        return c

    lax.fori_loop(0, nq // group, q_group, 0)


def _stick_breaking(q, k, v, b, tq_total, tk_total, tq, tk, qoff):
    assert tq_total % tq == 0 and tk_total % tk == 0 and tq <= tk and tk % tq == 0 and qoff % tk == 0
    assert qoff + tq_total <= tk_total
    w = H_B * HEAD_DIM
    q_spec = pl.BlockSpec((None, tq_total, HEAD_DIM), lambda bi, h: (bi, 0, h))
    kv_spec = pl.BlockSpec((None, tk_total, HEAD_DIM), lambda bi, h: (bi, 0, h))
    nq = tq_total // tq
    group = SB_GROUP if nq % SB_GROUP == 0 else 1
    return pl.pallas_call(
        functools.partial(_sb_kernel, nq=nq, tq=tq, tk=tk, qoff=qoff, group=group),
        out_shape=jax.ShapeDtypeStruct((b, tq_total, w), BF16),
        grid=(b, H_B),
        in_specs=[q_spec, kv_spec, kv_spec],
        out_specs=q_spec,
        scratch_shapes=[pltpu.VMEM((group, tq, HEAD_DIM), F32), pltpu.VMEM((group, tq, 1), F32)],
        compiler_params=_params("arbitrary", "arbitrary"),
        name="stick_breaking",
    )(q, k, v)


def _online_softmax_step(q, k, v, carry, valid):
    m, l, acc = carry
    s = lax.dot_general(k, q, _NT, preferred_element_type=F32)
    if valid is not None:
        s = jnp.where(valid, s, -jnp.inf)
    m_new = jnp.maximum(m, jnp.max(s, axis=0, keepdims=True))
    alpha = jnp.exp2(m - m_new)
    p = jnp.exp2(s - m_new)
    l = alpha * l + jnp.sum(p, axis=0, keepdims=True)
    acc = alpha * acc + lax.dot_general(v, p.astype(BF16), (((0,), (0,)), ((), ())),
                                        preferred_element_type=F32)
    return m_new, l, acc


def _mla_kernel(q_ref, k_ref, v_ref, o_ref, *, nq, tq, tk, td):
    key = lax.broadcasted_iota(jnp.int32, (td, td), 0)
    qry = lax.broadcasted_iota(jnp.int32, (td, td), 1)
    diag_valid = _chunk_of(key) <= _chunk_of(qry)

    def q_tile(qi, c):
        r0 = pl.multiple_of(qi * tq, tq)
        q = q_ref[pl.ds(r0, tq), :]

        def kv(j, carry):
            k0 = pl.multiple_of(j * tk, tk)
            return _online_softmax_step(q, k_ref[pl.ds(k0, tk), :], v_ref[pl.ds(k0, tk), :],
                                        carry, None)

        m, l, acc = lax.fori_loop(
            0, qi * (tq // tk), kv,
            (jnp.full((1, tq), -jnp.inf, F32), jnp.zeros((1, tq), F32), jnp.zeros((V_DIM, tq), F32)))
        for e in range(tq // td):
            rows = slice(e * td, (e + 1) * td)
            part = (m[:, rows], l[:, rows], acc[:, rows])
            qe = q[rows]
            if e > 0:
                k0 = pl.multiple_of(r0, td)
                part = _online_softmax_step(qe, k_ref[pl.ds(k0, e * td), :],
                                            v_ref[pl.ds(k0, e * td), :], part, None)
            k0 = pl.multiple_of(r0 + e * td, td)
            _, le, acce = _online_softmax_step(qe, k_ref[pl.ds(k0, td), :], v_ref[pl.ds(k0, td), :],
                                               part, diag_valid)
            o_ref[pl.ds(pl.multiple_of(r0 + e * td, td), td), :] = (acce / le).T.astype(o_ref.dtype)
        return c

    lax.fori_loop(0, nq, q_tile, 0)


def _latent_attention(q, k, v, b, s, tq=2048, tk=1024, td=1024):
    tq = min(tq, s)
    tk, td = min(tk, tq), min(td, tq)
    assert s % tq == 0 and tq % tk == 0 and tq % td == 0 and td % CHUNK == 0
    qk_spec = pl.BlockSpec((None, s, PADDED_HEAD), lambda bi, h: (bi, 0, h))
    v_spec = pl.BlockSpec((None, s, V_DIM), lambda bi, h: (bi, 0, h))
    out = pl.pallas_call(
        functools.partial(_mla_kernel, nq=s // tq, tq=tq, tk=tk, td=td),
        out_shape=jax.ShapeDtypeStruct((b, s, H_C * V_DIM), BF16),
        grid=(b, H_C),
        in_specs=[qk_spec, qk_spec, v_spec],
        out_specs=v_spec,
        compiler_params=_params("arbitrary", "arbitrary"),
        name="latent_attn",
    )(q.reshape(b, s, H_C * PADDED_HEAD), k.reshape(b, s, H_C * PADDED_HEAD),
      v.reshape(b, s, H_C * V_DIM))
    return out.reshape(b * s, H_C * V_DIM)


def _mla_sample_kernel(q_ref, k_ref, v_ref, o_ref, *, n_valid, qpos0):
    t, tk = q_ref.shape[0], k_ref.shape[0]
    row = lax.broadcasted_iota(jnp.int32, (t, tk), 0)
    col = lax.broadcasted_iota(jnp.int32, (t, tk), 1)
    valid = jnp.logical_and(col < n_valid, _chunk_of(col) <= _chunk_of(qpos0 + row))
    bias = jnp.where(valid, 0.0, -jnp.inf).astype(F32)
    for g in range(q_ref.shape[1] // PADDED_HEAD):
        qk = slice(g * PADDED_HEAD, (g + 1) * PADDED_HEAD)
        vo = slice(g * V_DIM, (g + 1) * V_DIM)
        o_ref[:, vo] = _softmax_attend(q_ref[:, qk], k_ref[:, qk], v_ref[:, vo],
                                       bias).astype(o_ref.dtype)


def _latent_attention_sample(q, k, v, bs, t, tk, n_valid, qpos0, heads_per_step=4):
    g = heads_per_step
    return pl.pallas_call(
        functools.partial(_mla_sample_kernel, n_valid=n_valid, qpos0=qpos0),
        out_shape=jax.ShapeDtypeStruct((bs * t, H_C * V_DIM), BF16),
        grid=(bs, H_C // g),
        in_specs=[pl.BlockSpec((t, g * PADDED_HEAD), lambda bi, h: (bi, h)),
                  pl.BlockSpec((None, tk, g * PADDED_HEAD), lambda bi, h: (bi, 0, h)),
                  pl.BlockSpec((None, tk, g * V_DIM), lambda bi, h: (bi, 0, h))],
        out_specs=pl.BlockSpec((t, g * V_DIM), lambda bi, h: (bi, h)),
        compiler_params=_params("arbitrary", "arbitrary"),
        name="latent_attn_sample",
    )(q, k, v)


def _rope_tables(pos):
    half = ROPE_DIM // 2
    inv = ROPE_THETA ** (-jnp.arange(half, dtype=F32) / half)
    ang = pos.astype(F32)[:, None] * inv[None, :]
    cos, sin = jnp.cos(ang), jnp.sin(ang)
    return (jnp.concatenate([cos, cos, cos, cos], -1), jnp.concatenate([-sin, sin, -sin, sin], -1))


def _swap_halves(w):
    half = ROPE_DIM // 2
    return jnp.concatenate([w[..., half:], w[..., :half]], -1)


def _rope_gains(g):
    z = jnp.zeros((LANES - g.shape[0],), g.dtype)
    return jnp.concatenate([g, z])[None, :], jnp.concatenate([z, g])[None, :]


class _Stream:
    def __init__(self, x, row0, pos0):
        self.b, self.s, self.d = x.shape
        self.m = self.b * self.s
        self.row0 = row0
        self.pos0 = pos0
        self.per_row_gate = self.s < 128


def _gate_operand(st, mod, ls):
    d = st.d
    if st.per_row_gate:
        g = jnp.repeat(mod[ls, st.row0:st.row0 + st.b, 2 * d:], st.s, axis=0)
        return g, lambda tm, tn: pl.BlockSpec((tm, tn), lambda i, j, kk: (i, j))
    mod4 = mod.reshape(mod.shape[0], mod.shape[1], 1, 3 * d)

    def spec(tm, tn):
        per_b = st.s // tm
        return pl.BlockSpec((None, None, 1, tn),
                            lambda i, j, kk: (ls, st.row0 + i // per_b, 0, 2 * d // tn + j))
    return mod4, spec


def _mlp_sublayer(st, x2d, mod, norm_g4, ls, w1, w2, layer):
    d = st.d
    mod4 = mod.reshape(mod.shape[0], mod.shape[1], 1, 3 * d)
    h = _adanorm(x2d.reshape(st.b, st.s, d), norm_g4, mod4, ls, st.row0)
    dff = w1.shape[-1]
    tm = _row_tile(st.m)
    tn = min(512, dff)
    res = _matmul(h, w1, 0, dff, tn, _epi_relu2, [],
                  [((st.m, dff), BF16, (tm, tn), lambda i, j: (i, j))], "mlp_up", layer=layer)
    if len(res) == 2:
        w1 = res[1]
    gate, gate_spec = _gate_operand(st, mod, ls)
    out = _resid_matmul(res[0], w2, x2d, gate, gate_spec, "mlp_down", layer=layer)
    if isinstance(out, tuple):
        out, w2 = out
    return out, (w1, w2)


def _even_layer(st, x2d, mod, norm_g4, ls, w_in, g_q, g_k, bias, w_out, cache):
    d = st.d
    m = st.m
    mod4 = mod.reshape(mod.shape[0], mod.shape[1], 1, 3 * d)
    h = _adanorm(x2d.reshape(st.b, st.s, d), norm_g4, mod4, ls, st.row0)
    tm = _row_tile(m)
    wa = H_A * HEAD_DIM
    wb = H_B * HEAD_DIM
    d_ab = wa + wb
    ns = 2 if tm >= 512 else 1
    tn = 512 * ns
    scale = HEAD_DIM ** -0.5 * LOG2E
    blk = lambda i, j: (i, j)
    bf = lambda: ((m, wa), BF16, (tm, tn), blk)
    f32 = lambda: ((m, wa), F32, (tm, tn), blk)

    def gain(g):
        return [(g[None, :], (1, HEAD_DIM), lambda i, j: (0, 0))]

    reuse = {}

    def proj(col0, epi, extras, outs, name):
        w, c0 = (w_in[name], 0) if isinstance(w_in, dict) else (w_in, col0)
        res = _matmul(h, w, c0, wa, tn, epi, extras, outs, name, n_split=ns)
        if len(res) > len(outs):
            reuse[name] = res[-1]
        return res[:len(outs)]

    qa, = proj(0, functools.partial(_epi_headnorm, scale=scale), gain(g_q), [bf()], "proj_qa")
    qb, = proj(wa, functools.partial(_epi_plain, scale=scale), [], [bf()], "proj_qb")
    ka, ka32 = proj(d_ab, _epi_headnorm, gain(g_k), [bf(), f32()], "proj_ka")
    kb, kb32 = proj(d_ab + wa, _epi_plain, [], [bf(), f32()], "proj_kb")
    va, va32 = proj(2 * d_ab, _epi_plain, [], [bf(), f32()], "proj_va")
    vb, vb32 = proj(2 * d_ab + wa, _epi_plain, [], [bf(), f32()], "proj_vb")

    if cache is None:
        oa = _band_attention(qa, ka, va, bias, st.b, st.s)
        t = min(MXU_DIM, st.s)
        v3 = lambda a: a.reshape(st.b, st.s, wb)
        ob = _stick_breaking(v3(qb), v3(kb), v3(vb), st.b, st.s, st.s, t, t, 0).reshape(m, wb)
    else:
        ca_k, ca_v, cb_k, cb_v = cache
        n_keep = ca_k.shape[1]
        assert st.s <= CHUNK and n_keep + st.s <= BAND_W and n_keep == BAND_PAST

        def with_cache(c, new, total):
            c = c.reshape(st.b, c.shape[1], -1).astype(BF16)
            new = new.reshape(st.b, st.s, -1)
            pad = jnp.zeros((st.b, total - c.shape[1] - st.s, c.shape[2]), BF16)
            return jnp.concatenate([c, new, pad], 1)

        oa = _band_attention_sample(qa, with_cache(ca_k, ka, BAND_W), with_cache(ca_v, va, BAND_W),
                                    bias, st.b, st.s, n_keep + st.s)
        past = cb_k.shape[1]
        tk = MXU_DIM
        assert past % tk == 0
        total = past + tk
        ob = _stick_breaking(qb.reshape(st.b, st.s, wb), with_cache(cb_k, kb, total),
                             with_cache(cb_v, vb, total), st.b, st.s, total, st.s, tk, past)
        ob = ob.reshape(m, wb)

    gate, gate_spec = _gate_operand(st, mod, ls)
    x2d = _resid_matmul([oa, ob], w_out, x2d, gate, gate_spec, "out_ab")
    if isinstance(x2d, tuple):
        x2d, reuse["out"] = x2d
    r4 = lambda a, hh: a.reshape(st.b, st.s, hh, HEAD_DIM)
    return x2d, (r4(ka32, H_A), r4(va32, H_A), r4(kb32, H_B), r4(vb32, H_B)), reuse


def _odd_layer(st, x2d, mod, norm_g4, ls, wq_a, g_q_lat, wq_b, wkv_c, wkv_pe, g_kv_lat, wkv_bk, wkv_bv,
               g_q, g_k, w_out, cache):
    d = st.d
    m = st.m
    mod4 = mod.reshape(mod.shape[0], mod.shape[1], 1, 3 * d)
    h = _adanorm(x2d.reshape(st.b, st.s, d), norm_g4, mod4, ls, st.row0)
    tm = _row_tile(m)
    blk = lambda i, j: (i, j)
    row_blk = lambda i, j: (i, 0)
    one = lambda i, j: (0, 0)
    q_lora = wq_a.shape[1]
    kv_lora = wkv_c.shape[1]

    cos, sin = _rope_tables(st.pos0 + jnp.arange(st.s))
    per_b = st.s // tm if st.s >= tm else 0
    if per_b:
        tab_blk = lambda i, j: (i % per_b, 0)
    else:
        cos, sin = jnp.tile(cos, (st.b, 1)), jnp.tile(sin, (st.b, 1))
        tab_blk = row_blk
    tables = [(cos, (tm, LANES), tab_blk), (sin, (tm, LANES), tab_blk)]

    q_lat, = _matmul(h, wq_a, 0, q_lora, q_lora, _epi_rownorm,
                     [(g_q_lat[None, :], (1, q_lora), one)],
                     [((m, q_lora), BF16, (tm, q_lora), blk)], "proj_q_lat")
    c_kv, c_kv32 = _matmul(h, wkv_c, 0, kv_lora, kv_lora, _epi_rownorm,
                           [(g_kv_lat[None, :], (1, kv_lora), one)],
                           [((m, kv_lora), BF16, (tm, kv_lora), blk),
                            ((m, kv_lora), F32, (tm, kv_lora), blk)], "proj_c_kv")
    k_pe, = _matmul(h, wkv_pe, 0, 2 * LANES, 2 * LANES, _epi_rope_shared, tables,
                    [((m, 2 * LANES), F32, (tm, 2 * LANES), blk)], "proj_k_pe")

    wide = H_C * PADDED_HEAD
    gq_n, gq_r = g_q[None, :NOPE_DIM], _rope_gains(g_q[NOPE_DIM:])
    gk_n, gk_r = g_k[None, :NOPE_DIM], _rope_gains(g_k[NOPE_DIM:])
    lane_vec = lambda g: (g, (1, LANES), one)
    ns = 2 if tm >= 512 else 1
    q, = _matmul(q_lat, wq_b, 0, wide, 512 * ns,
                 functools.partial(_epi_q_latent, scale=QK_DIM ** -0.5 * LOG2E),
                 tables + [lane_vec(gq_n), lane_vec(gq_r[0]), lane_vec(gq_r[1])],
                 [((m, wide), BF16, (tm, 512 * ns), blk)], "proj_q", n_split=ns)

    if cache is None:
        c_all, pe_all, mk = c_kv, k_pe, m
    else:
        cc_kv, cc_pe = cache
        past = cc_kv.shape[1]
        total = past + MXU_DIM
        pad = total - past - st.s
        c_all = jnp.concatenate([cc_kv.astype(BF16), c_kv.reshape(st.b, st.s, kv_lora),
                                 jnp.zeros((st.b, pad, kv_lora), BF16)], 1)
        gap = jnp.zeros(cc_pe.shape[:2] + (2 * (LANES - ROPE_DIM),), F32)
        pe_c = jnp.concatenate([cc_pe, gap, cc_pe], -1)
        pe_all = jnp.concatenate([pe_c, k_pe.reshape(st.b, st.s, 2 * LANES),
                                  jnp.zeros((st.b, pad, 2 * LANES), F32)], 1)
        mk = st.b * total
        c_all, pe_all = c_all.reshape(mk, kv_lora), pe_all.reshape(mk, 2 * LANES)

    tmk = _row_tile(mk)
    nk = H_C * NOPE_DIM
    nsk = 2 if tmk >= 512 else 1
    k, = _matmul(c_all, wkv_bk, 0, nk, 512 * nsk, _epi_k_latent,
                 [(pe_all, (tmk, 2 * LANES), row_blk), lane_vec(gk_n), lane_vec(gk_r[0]),
                  lane_vec(gk_r[1])],
                 [((mk, wide), BF16, (tmk, 2 * 512 * nsk), blk)], "proj_k", n_split=nsk)
    v, = _matmul(c_all, wkv_bv, 0, H_C * V_DIM, 512, _epi_plain, [],
                 [((mk, H_C * V_DIM), BF16, (tmk, 512), blk)], "proj_v")

    if cache is None:
        o = _latent_attention(q, k, v, st.b, st.s)
    else:
        o = _latent_attention_sample(q, k.reshape(st.b, total, wide),
                                     v.reshape(st.b, total, H_C * V_DIM),
                                     st.b, st.s, total, past + st.s, st.pos0)
    gate, gate_spec = _gate_operand(st, mod, ls)
    x2d = _resid_matmul(o, w_out, x2d, gate, gate_spec, "out_c")
    if isinstance(x2d, tuple):
        x2d, w_out = x2d
    keep = (c_kv32.reshape(st.b, st.s, kv_lora), k_pe[:, :ROPE_DIM].reshape(st.b, st.s, ROPE_DIM))
    return x2d, keep, w_out


def kernel(x_prompt, x_sample, cache_a_k, cache_a_v, cache_b_k, cache_b_v, cache_c_kv, cache_c_pe,
           c_prompt, c_sample, norm_g, ada_w, ada_b, w_in_ab, g_q_a, g_k_a, rel_bias_a, w_out_ab,
           wq_a_c, g_q_lat_c, wq_b_c, wkv_a_c, g_kv_lat_c, wkv_b_c, g_q_c, g_k_c, w_out_c,
           mlp_w1, mlp_w2):
    bp, s, d = x_prompt.shape
    bs, t, _ = x_sample.shape
    depth = norm_g.shape[0]
    past_len = cache_b_k.shape[2]
    n_keep = cache_a_k.shape[2]
    assert s >= n_keep

    n_rows = -(-(bp + bs) // 16) * 16
    c_rows = jnp.concatenate([c_prompt, c_sample, jnp.zeros((n_rows - bp - bs, d), F32)], 0)
    mod = _ada_params(c_rows, ada_w.reshape(depth * 2, d, 3 * d), ada_b.reshape(depth * 2, 1, 3 * d))
    norm_g4 = norm_g.reshape(depth * 2, 1, d)

    prompt = _Stream(x_prompt, 0, 0)
    sample = _Stream(x_sample, bp, past_len)
    xp = x_prompt.reshape(bp * s, d)
    xs = x_sample.reshape(bs * t, d)

    outs_p = {"a": [], "c": []}
    outs_s = {"a": [], "c": []}
    for layer in range(depth):
        ls = 2 * layer
        if layer % 2 == 0:
            e = layer // 2
            bias = _band_bias(rel_bias_a[e])
            xs, keep, w_bf16 = _even_layer(
                sample, xs, mod, norm_g4, ls, w_in_ab[e], g_q_a[e], g_k_a[e], bias, w_out_ab[e],
                (cache_a_k[e], cache_a_v[e], cache_b_k[e], cache_b_v[e]))
            outs_s["a"].append(keep)
            xp, keep, _ = _even_layer(prompt, xp, mod, norm_g4, ls, w_bf16, g_q_a[e], g_k_a[e], bias,
                                      w_bf16["out"], None)
            outs_p["a"].append(keep)
        else:
            mi = layer // 2
            kv_lora = g_kv_lat_c.shape[1]
            wq_a = wq_a_c[mi].astype(BF16)
            wq_b = wq_b_c[mi].astype(BF16).reshape(-1, H_C // 2, 2, QK_DIM)
            q_lora = wq_b.shape[0]
            wq_n = wq_b[..., :NOPE_DIM].reshape(q_lora, H_C // 2, 2 * NOPE_DIM)
            wq_r = wq_b[..., NOPE_DIM:]
            wq_b = jnp.concatenate([wq_n, wq_r.reshape(q_lora, H_C // 2, 2 * ROPE_DIM),
                                    _swap_halves(wq_r).reshape(q_lora, H_C // 2, 2 * ROPE_DIM)], -1)
            wq_b = wq_b.reshape(q_lora, H_C * PADDED_HEAD)
            wkv_a = wkv_a_c[mi]
            wkv_c = wkv_a[:, :kv_lora].astype(BF16)
            wkv_pe = wkv_a[:, kv_lora:].astype(BF16)
            wkv_sw = _swap_halves(wkv_pe)
            wkv_pe = jnp.concatenate([wkv_pe, wkv_sw, wkv_sw, wkv_pe], -1)
            wkv_b = wkv_b_c[mi].reshape(kv_lora, H_C, NOPE_DIM + V_DIM)
            wkv_bk = wkv_b[..., :NOPE_DIM].reshape(kv_lora, H_C * NOPE_DIM).astype(BF16)
            wkv_bv = wkv_b[..., NOPE_DIM:].reshape(kv_lora, H_C * V_DIM).astype(BF16)
            args = (wq_a, g_q_lat_c[mi], wq_b, wkv_c, wkv_pe, g_kv_lat_c[mi], wkv_bk, wkv_bv,
                    g_q_c[mi], g_k_c[mi])
            xs, keep, w_out = _odd_layer(sample, xs, mod, norm_g4, ls, *args, w_out_c[mi],
                                         (cache_c_kv[mi], cache_c_pe[mi]))
            outs_s["c"].append(keep)
            xp, keep, _ = _odd_layer(prompt, xp, mod, norm_g4, ls, *args, w_out, None)
            outs_p["c"].append(keep)
        xs, (w1, w2) = _mlp_sublayer(sample, xs, mod, norm_g4, ls + 1, mlp_w1, mlp_w2, layer)
        xp, _ = _mlp_sublayer(prompt, xp, mod, norm_g4, ls + 1, w1, w2, layer)

    stack = lambda items, i: jnp.stack([it[i] for it in items])
    tail = lambda a: a[:, :, s - n_keep:]
    return (xp.reshape(bp, s, d), xs.reshape(bs, t, d),
            tail(stack(outs_p["a"], 0)), tail(stack(outs_p["a"], 1)),
            stack(outs_s["a"], 0), stack(outs_s["a"], 1),
            stack(outs_p["a"], 2), stack(outs_p["a"], 3),
            stack(outs_s["a"], 2), stack(outs_s["a"], 3),
            stack(outs_p["c"], 0), stack(outs_p["c"], 1),
            stack(outs_s["c"], 0), stack(outs_s["c"], 1))
```

```python
import functools
import math

import jax
import jax.numpy as jnp
from jax import lax
from jax.experimental import pallas as pl
from jax.experimental.pallas import tpu as pltpu

CHUNK = 64
BAND_CHUNKS = 8
BAND_PAST = BAND_CHUNKS * CHUNK
HEAD_DIM = 128
H_A = 16
H_B = 16
REL_CLIP = 128
N_REL = 2 * REL_CLIP + 1
H_C = 32
NOPE_DIM = 128
ROPE_DIM = 64
QK_DIM = NOPE_DIM + ROPE_DIM
V_DIM = 128
ROPE_THETA = 10000.0
EPS = 1e-6

LANES = 128
MXU_DIM = 256
VMEM_LIMIT_BYTES = 52 * 1024 * 1024

SB_EXIT = 152.0
SB_GROUP = 4

BAND_Q = 2 * CHUNK
BAND_W = BAND_PAST + BAND_Q
BAND_UNROLL = 10
PADDED_HEAD = 2 * LANES

LOG2E = math.log2(math.e)
F32 = jnp.float32
BF16 = jnp.bfloat16
_NT = (((1,), (1,)), ((), ()))


def _chunk_of(pos):
    return jnp.right_shift(pos, int(math.log2(CHUNK)))


def _params(*sem):
    return pltpu.CompilerParams(dimension_semantics=sem, vmem_limit_bytes=VMEM_LIMIT_BYTES)


def _row_tile(m, cap=1024):
    t = min(m, cap)
    while m % t:
        t //= 2
    return t


def _ada_kernel(c_ref, w_ref, b_ref, o_ref, *, d_tiles):
    j = pl.program_id(1)
    c = c_ref[...]
    a = (c * jax.nn.sigmoid(c)).astype(BF16)
    acc = jnp.dot(a, w_ref[...].astype(BF16), preferred_element_type=F32) + b_ref[...]
    is_scale = jnp.logical_and(j >= d_tiles, j < 2 * d_tiles)
    o_ref[...] = acc + jnp.where(is_scale, 1.0, 0.0).astype(F32)


def _ada_params(c_rows, ada_w, ada_b):
    n_l, d, d3 = ada_w.shape
    r = c_rows.shape[0]
    tn = min(512, d)
    return pl.pallas_call(
        functools.partial(_ada_kernel, d_tiles=d // tn),
        out_shape=jax.ShapeDtypeStruct((n_l, r, d3), F32),
        grid=(n_l, d3 // tn),
        in_specs=[pl.BlockSpec((r, d), lambda l, j: (0, 0)),
                  pl.BlockSpec((None, d, tn), lambda l, j: (l, 0, j)),
                  pl.BlockSpec((None, 1, tn), lambda l, j: (l, 0, j))],
        out_specs=pl.BlockSpec((None, r, tn), lambda l, j: (l, 0, j)),
        compiler_params=_params("arbitrary", "arbitrary"),
        name="ada_params",
    )(c_rows, ada_w, ada_b)


def _adanorm_kernel(x_ref, g_ref, sh_ref, sc_ref, o_ref):
    x = x_ref[...]
    ms = jnp.mean(x * x, axis=-1, keepdims=True)
    y = x * lax.rsqrt(ms + EPS) * g_ref[...]
    o_ref[...] = (y * sc_ref[...] + sh_ref[...]).astype(o_ref.dtype)


def _adanorm(x, g, mod4, ls, row0):
    b, s, d = x.shape
    tm = _row_tile(s, 512)
    nt = s // tm
    return pl.pallas_call(
        _adanorm_kernel,
        out_shape=jax.ShapeDtypeStruct((b * s, d), BF16),
        grid=(b, nt),
        in_specs=[pl.BlockSpec((None, tm, d), lambda bi, i: (bi, i, 0)),
                  pl.BlockSpec((None, 1, d), lambda bi, i: (ls, 0, 0)),
                  pl.BlockSpec((None, None, 1, d), lambda bi, i: (ls, row0 + bi, 0, 0)),
                  pl.BlockSpec((None, None, 1, d), lambda bi, i: (ls, row0 + bi, 0, 1))],
        out_specs=pl.BlockSpec((tm, d), lambda bi, i: (bi * nt + i, 0)),
        compiler_params=_params("arbitrary", "arbitrary"),
        name="adanorm",
    )(x, g, mod4, mod4)


def _mm_kernel(*refs, n_extra, epilogue, n_split, emit_w):
    x_ref, w_ref = refs[:2]
    extra = refs[2:2 + n_extra]
    outs = refs[2 + n_extra:]
    if emit_w:
        w = w_ref[...].astype(BF16)
        outs[-1][...] = w
        epilogue(jnp.dot(x_ref[...], w, preferred_element_type=F32), extra, outs[:-1])
        return
    if n_split == 1:
        epilogue(jnp.dot(x_ref[...], w_ref[...], preferred_element_type=F32), extra, outs)
        return
    sub = w_ref.shape[1] // n_split
    for c in range(n_split):
        acc = jnp.dot(x_ref[...], w_ref[:, c * sub:(c + 1) * sub], preferred_element_type=F32)
        views = []
        for o in outs:
            width = o.shape[1] // n_split
            views.append(o.at[:, pl.ds(c * width, width)])
        epilogue(acc, extra, views)


def _matmul(x, w, col0, ncols, tn, epilogue, extras, outs, name, layer=0, n_split=1):
    m, k = x.shape
    tm = _row_tile(m)
    assert ncols % tn == 0 and col0 % tn == 0
    j0 = col0 // tn
    emit_w = w.dtype == F32
    if emit_w:
        assert m == tm and n_split == 1
        outs = list(outs) + [((k, ncols), BF16, (k, tn), lambda i, j: (0, j))]
    if w.ndim == 3:
        w_spec = pl.BlockSpec((None, k, tn), lambda i, j: (layer, 0, j0 + j))
    else:
        w_spec = pl.BlockSpec((k, tn), lambda i, j: (0, j0 + j))
    in_specs = [pl.BlockSpec((tm, k), lambda i, j: (i, 0)), w_spec]
    in_specs += [pl.BlockSpec(bs, im) for _, bs, im in extras]
    return pl.pallas_call(
        functools.partial(_mm_kernel, n_extra=len(extras), epilogue=epilogue, n_split=n_split,
                          emit_w=emit_w),
        out_shape=[jax.ShapeDtypeStruct(sh, dt) for sh, dt, _, _ in outs],
        grid=(m // tm, ncols // tn),
        in_specs=in_specs,
        out_specs=[pl.BlockSpec(bs, im) for _, _, bs, im in outs],
        compiler_params=_params("arbitrary", "arbitrary"),
        name=name,
    )(x, w, *[a for a, _, _ in extras])


def _epi_plain(acc, extra, outs, *, scale=1.0):
    for o in outs:
        o[...] = (acc * scale if scale != 1.0 else acc).astype(o.dtype)


def _epi_relu2(acc, extra, outs):
    outs[0][...] = jnp.square(jnp.maximum(acc, 0.0)).astype(outs[0].dtype)


def _epi_headnorm(acc, extra, outs, *, scale=1.0):
    g = extra[0][...]
    for h in range(acc.shape[1] // HEAD_DIM):
        sl = slice(h * HEAD_DIM, (h + 1) * HEAD_DIM)
        a = acc[:, sl]
        y = a * lax.rsqrt(jnp.mean(a * a, axis=-1, keepdims=True) + EPS) * g
        if scale != 1.0:
            y = y * scale
        for o in outs:
            o[:, sl] = y.astype(o.dtype)


def _epi_rownorm(acc, extra, outs):
    g = extra[0][...]
    y = acc * lax.rsqrt(jnp.mean(acc * acc, axis=-1, keepdims=True) + EPS) * g
    for o in outs:
        o[...] = y.astype(o.dtype)


def _epi_rope_shared(acc, extra, outs):
    cos, sin = extra[0][...], extra[1][...]
    a, b = acc[:, :LANES], acc[:, LANES:]
    lo = lax.broadcasted_iota(jnp.int32, a.shape, 1) < ROPE_DIM
    outs[0][:, :LANES] = jnp.where(lo, a * cos + b * sin, 0.0)
    outs[0][:, LANES:] = jnp.where(lo, 0.0, b * cos + a * sin)


def _epi_q_latent(acc, extra, outs, *, scale):
    cos, sin = extra[0][...], extra[1][...]
    gn, gr = extra[2][...], (extra[3][...], extra[4][...])
    pair = 2 * PADDED_HEAD
    lo = lax.broadcasted_iota(jnp.int32, cos.shape, 1) < ROPE_DIM
    for p in range(acc.shape[1] // pair):
        c0 = p * pair
        r = (acc[:, c0 + 2 * NOPE_DIM:c0 + 3 * NOPE_DIM] * cos
             + acc[:, c0 + 3 * NOPE_DIM:c0 + 4 * NOPE_DIM] * sin)
        r2 = r * r
        for h in range(2):
            n = acc[:, c0 + h * NOPE_DIM:c0 + (h + 1) * NOPE_DIM]
            mine = lo if h == 0 else jnp.logical_not(lo)
            ss = jnp.sum(n * n + jnp.where(mine, r2, 0.0), axis=-1, keepdims=True)
            inv = lax.rsqrt(ss * (1.0 / QK_DIM) + EPS) * scale
            o0 = c0 + h * PADDED_HEAD
            outs[0][:, o0:o0 + NOPE_DIM] = (n * inv * gn).astype(BF16)
            outs[0][:, o0 + NOPE_DIM:o0 + PADDED_HEAD] = (r * inv * gr[h]).astype(BF16)


def _epi_k_latent(acc, extra, outs):
    pe = (extra[0][:, :LANES], extra[0][:, LANES:])
    gn, gr = extra[1][...], (extra[2][...], extra[3][...])
    pe2 = pe[0] * pe[0]
    for h in range(acc.shape[1] // NOPE_DIM):
        n = acc[:, h * NOPE_DIM:(h + 1) * NOPE_DIM]
        ss = jnp.sum(n * n + pe2, axis=-1, keepdims=True)
        inv = lax.rsqrt(ss * (1.0 / QK_DIM) + EPS)
        c0 = h * PADDED_HEAD
        outs[0][:, c0:c0 + NOPE_DIM] = (n * inv * gn).astype(BF16)
        outs[0][:, c0 + NOPE_DIM:c0 + PADDED_HEAD] = (pe[h % 2] * inv * gr[h % 2]).astype(BF16)


def _resid_kernel(*refs, nk, n_a, emit_w):
    a_refs = refs[:n_a]
    w_ref, x_ref, g_ref, o_ref = refs[n_a:n_a + 4]
    rest = refs[n_a + 4:]
    w = w_ref[...]
    if emit_w:
        w = w.astype(BF16)
        rest[0][...] = w
        rest = rest[1:]
    if nk == 1:
        acc = jnp.dot(a_refs[0][...], w, preferred_element_type=F32)
        o_ref[...] = x_ref[...] + g_ref[...] * acc
        return
    acc_ref, = rest
    k = pl.program_id(2)

    @pl.when(k == 0)
    def _():
        acc_ref[...] = jnp.zeros_like(acc_ref)

    if n_a == 1:
        acc_ref[...] += jnp.dot(a_refs[0][...], w, preferred_element_type=F32)
    else:
        for idx, a_ref in enumerate(a_refs):
            @pl.when(k == idx)
            def _(a_ref=a_ref):
                acc_ref[...] += jnp.dot(a_ref[...], w, preferred_element_type=F32)

    @pl.when(k == nk - 1)
    def _():
        o_ref[...] = x_ref[...] + g_ref[...] * acc_ref[...]


def _resid_matmul(a, w, x, gate, gate_spec, name, layer=0):
    a_list = list(a) if isinstance(a, (list, tuple)) else [a]
    m = a_list[0].shape[0]
    d = w.shape[-1]
    tm = _row_tile(m)
    tn = min(1024, d)
    if len(a_list) == 1:
        k = a_list[0].shape[1]
        tk = min(2048, k)
        nk = k // tk
        a_specs = [pl.BlockSpec((tm, tk), lambda i, j, kk: (i, kk))]
    else:
        tk = a_list[0].shape[1]
        nk = len(a_list)
        assert all(t.shape == (m, tk) for t in a_list)
        a_specs = [pl.BlockSpec((tm, tk), lambda i, j, kk: (i, 0)) for _ in a_list]
    scratch = [] if nk == 1 else [pltpu.VMEM((tm, tn), F32)]
    if w.ndim == 3:
        w_spec = pl.BlockSpec((None, tk, tn), lambda i, j, kk: (layer, kk, j))
    else:
        w_spec = pl.BlockSpec((tk, tn), lambda i, j, kk: (kk, j))
    out_shape = [jax.ShapeDtypeStruct((m, d), F32)]
    out_specs = [pl.BlockSpec((tm, tn), lambda i, j, kk: (i, j))]
    emit_w = w.dtype == F32
    if emit_w:
        assert m == tm
        out_shape.append(jax.ShapeDtypeStruct((nk * tk, d), BF16))
        out_specs.append(pl.BlockSpec((tk, tn), lambda i, j, kk: (kk, j)))
    res = pl.pallas_call(
        functools.partial(_resid_kernel, nk=nk, n_a=len(a_list), emit_w=emit_w),
        out_shape=out_shape,
        grid=(m // tm, d // tn, nk),
        in_specs=a_specs + [w_spec,
                            pl.BlockSpec((tm, tn), lambda i, j, kk: (i, j)),
                            gate_spec(tm, tn)],
        out_specs=out_specs,
        scratch_shapes=scratch,
        compiler_params=_params("arbitrary", "arbitrary", "arbitrary"),
        name=name,
    )(*a_list, w, x, gate)
    return tuple(res) if emit_w else res[0]


def _bias_kernel(tab_ref, o_ref):
    h = pl.program_id(0)
    row = lax.broadcasted_iota(jnp.int32, (BAND_Q, BAND_W), 0)
    col = lax.broadcasted_iota(jnp.int32, (BAND_Q, BAND_W), 1)
    idx = jnp.clip(BAND_PAST + row - col, -REL_CLIP, REL_CLIP) + REL_CLIP
    rel = BAND_CHUNKS + _chunk_of(row) - _chunk_of(col)
    valid = jnp.logical_and(rel >= 0, rel <= BAND_CHUNKS)

    def body(t, acc):
        return jnp.where(idx == t, tab_ref[h, t], acc)

    acc = lax.fori_loop(0, N_REL, body, jnp.zeros((BAND_Q, BAND_W), F32))
    o_ref[...] = jnp.where(valid, acc * LOG2E, -jnp.inf)


def _band_bias(rel_bias):
    return pl.pallas_call(
        _bias_kernel,
        out_shape=jax.ShapeDtypeStruct((H_A, BAND_Q, BAND_W), F32),
        grid=(H_A,),
        in_specs=[pl.BlockSpec(memory_space=pltpu.SMEM)],
        out_specs=pl.BlockSpec((None, BAND_Q, BAND_W), lambda h: (h, 0, 0)),
        compiler_params=_params("arbitrary"),
        name="band_bias",
    )(rel_bias)


def _softmax_attend(q, k, v, bias):
    s = lax.dot_general(q, k, _NT, preferred_element_type=F32) + bias
    m = jnp.max(s, axis=-1, keepdims=True)
    p = jnp.exp2(s - m)
    l = jnp.sum(p, axis=-1, keepdims=True)
    o = jnp.dot(p.astype(BF16), v, preferred_element_type=F32)
    return o / l


def _band_kernel(q_ref, k_ref, v_ref, bias_ref, o_ref, *, nq):
    lead = BAND_PAST // BAND_Q
    for i in range(min(lead, nq)):
        rows = slice(i * BAND_Q, (i + 1) * BAND_Q)
        keys = slice(0, (i + 1) * BAND_Q)
        bias = bias_ref[:, (lead - i) * BAND_Q:]
        o_ref[rows, :] = _softmax_attend(q_ref[rows, :], k_ref[keys, :], v_ref[keys, :],
                                         bias).astype(o_ref.dtype)

    def body(i, c):
        r0 = pl.multiple_of(i * BAND_Q, BAND_Q)
        k0 = pl.multiple_of((i - lead) * BAND_Q, BAND_Q)
        o = _softmax_attend(q_ref[pl.ds(r0, BAND_Q), :], k_ref[pl.ds(k0, BAND_W), :],
                            v_ref[pl.ds(k0, BAND_W), :], bias_ref[...])
        o_ref[pl.ds(r0, BAND_Q), :] = o.astype(o_ref.dtype)
        return c

    if nq > lead:
        unroll = BAND_UNROLL if (nq - lead) % BAND_UNROLL == 0 else 1
        lax.fori_loop(lead, nq, body, 0, unroll=unroll)


def _band_attention(q, k, v, bias, b, s):
    assert s % BAND_Q == 0
    view = lambda t: t.reshape(b, s, H_A * HEAD_DIM)
    spec = pl.BlockSpec((None, s, HEAD_DIM), lambda bi, h: (bi, 0, h))
    out = pl.pallas_call(
        functools.partial(_band_kernel, nq=s // BAND_Q),
        out_shape=jax.ShapeDtypeStruct((b, s, H_A * HEAD_DIM), BF16),
        grid=(b, H_A),
        in_specs=[spec, spec, spec,
                  pl.BlockSpec((None, BAND_Q, BAND_W), lambda bi, h: (h, 0, 0))],
        out_specs=spec,
        compiler_params=_params("arbitrary", "arbitrary"),
        name="band_attn",
    )(view(q), view(k), view(v), bias)
    return out.reshape(b * s, H_A * HEAD_DIM)


def _band_sample_kernel(q_ref, k_ref, v_ref, bias_ref, o_ref, *, n_valid):
    t = q_ref.shape[0]
    col = lax.broadcasted_iota(jnp.int32, (t, BAND_W), 1)
    bias = jnp.where(col < n_valid, bias_ref[0:t, :], -jnp.inf)
    o_ref[...] = _softmax_attend(q_ref[...], k_ref[...], v_ref[...], bias).astype(o_ref.dtype)


def _band_attention_sample(q, kk, vv, bias, bs, t, n_valid):
    kv_spec = pl.BlockSpec((None, BAND_W, HEAD_DIM), lambda bi, h: (bi, 0, h))
    q_spec = pl.BlockSpec((t, HEAD_DIM), lambda bi, h: (bi, h))
    return pl.pallas_call(
        functools.partial(_band_sample_kernel, n_valid=n_valid),
        out_shape=jax.ShapeDtypeStruct((bs * t, H_A * HEAD_DIM), BF16),
        grid=(bs, H_A),
        in_specs=[q_spec, kv_spec, kv_spec,
                  pl.BlockSpec((None, BAND_Q, BAND_W), lambda bi, h: (h, 0, 0))],
        out_specs=q_spec,
        compiler_params=_params("arbitrary", "arbitrary"),
        name="band_attn_sample",
    )(q, kk, vv, bias)


def _sb_kernel(q_ref, k_ref, v_ref, o_ref, acc_ref, run_ref, *, nq, tq, tk, qoff, group):
    above = (lax.broadcasted_iota(jnp.int32, (tk, tk), 0)
             > lax.broadcasted_iota(jnp.int32, (tk, tk), 1)).astype(BF16)

    def block(q, j, run, causal):
        k0 = pl.multiple_of(j * tk, tk)
        z = lax.dot_general(q, k_ref[pl.ds(k0, tk), :], _NT, preferred_element_type=F32)
        log_sig = jnp.minimum(z, 0.0) - jnp.log2(1.0 + jnp.exp2(-jnp.abs(z)))
        log_keep = log_sig - z
        if causal is not None:
            log_keep = jnp.where(causal, log_keep, 0.0)
        after = jnp.dot(log_keep.astype(BF16), above, preferred_element_type=F32) + run
        w = jnp.exp2(log_sig + after)
        if causal is not None:
            w = jnp.where(causal, w, 0.0)
        o = jnp.dot(w.astype(BF16), v_ref[pl.ds(k0, tk), :], preferred_element_type=F32)
        return o, run + jnp.sum(log_keep, axis=-1, keepdims=True)

    row = lax.broadcasted_iota(jnp.int32, (tq, tk), 0)
    col = lax.broadcasted_iota(jnp.int32, (tq, tk), 1)

    def nearest_blocks(qi, u):
        r0 = pl.multiple_of(qi * tq, tq)
        q = q_ref[pl.ds(r0, tq), :]
        jd = (qoff + r0) // tk
        o, run = block(q, jd, jnp.zeros((tq, 1), F32), jd * tk + col < qoff + r0 + row)
        o_prev, run = block(q, jnp.maximum(jd - 1, 0), run, col < jnp.where(jd > 0, tk, 0))
        acc_ref[u] = o + o_prev
        run_ref[u] = run
        return jd, jnp.max(run)

    def remaining_blocks(qi, u, jd, run_max):
        r0 = pl.multiple_of(qi * tq, tq)

        def cond(st):
            j, run_max = st
            return jnp.logical_and(j >= 0, run_max > -SB_EXIT)

        def body(st):
            j, _ = st
            o, run = block(q_ref[pl.ds(r0, tq), :], j, run_ref[u], None)
            acc_ref[u] += o
            run_ref[u] = run
            return j - 1, jnp.max(run)

        lax.while_loop(cond, body, (jd - 2, run_max))
        o_ref[pl.ds(r0, tq), :] = acc_ref[u].astype(o_ref.dtype)

    def q_group(gi, c):
        state = [nearest_blocks(gi * group + u, u) for u in range(group)]
        for u, (jd, run_max) in enumerate(state):
            remaining_blocks(gi * group + u, u, jd, run_max)
        return c

    lax.fori_loop(0, nq // group, q_group, 0)


def _stick_breaking(q, k, v, b, tq_total, tk_total, tq, tk, qoff):
    assert tq_total % tq == 0 and tk_total % tk == 0 and tq <= tk and tk % tq == 0 and qoff % tk == 0
    assert qoff + tq_total <= tk_total
    w = H_B * HEAD_DIM
    q_spec = pl.BlockSpec((None, tq_total, HEAD_DIM), lambda bi, h: (bi, 0, h))
    kv_spec = pl.BlockSpec((None, tk_total, HEAD_DIM), lambda bi, h: (bi, 0, h))
    nq = tq_total // tq
    group = SB_GROUP if nq % SB_GROUP == 0 else 1
    return pl.pallas_call(
        functools.partial(_sb_kernel, nq=nq, tq=tq, tk=tk, qoff=qoff, group=group),
        out_shape=jax.ShapeDtypeStruct((b, tq_total, w), BF16),
        grid=(b, H_B),
        in_specs=[q_spec, kv_spec, kv_spec],
        out_specs=q_spec,
        scratch_shapes=[pltpu.VMEM((group, tq, HEAD_DIM), F32), pltpu.VMEM((group, tq, 1), F32)],
        compiler_params=_params("arbitrary", "arbitrary"),
        name="stick_breaking",
    )(q, k, v)


def _online_softmax_step(q, k, v, carry, valid):
    m, l, acc = carry
    s = lax.dot_general(k, q, _NT, preferred_element_type=F32)
    if valid is not None:
        s = jnp.where(valid, s, -jnp.inf)
    m_new = jnp.maximum(m, jnp.max(s, axis=0, keepdims=True))
    alpha = jnp.exp2(m - m_new)
    p = jnp.exp2(s - m_new)
    l = alpha * l + jnp.sum(p, axis=0, keepdims=True)
    acc = alpha * acc + lax.dot_general(v, p.astype(BF16), (((0,), (0,)), ((), ())),
                                        preferred_element_type=F32)
    return m_new, l, acc


def _mla_kernel(q_ref, k_ref, v_ref, o_ref, *, nq, tq, tk, td):
    key = lax.broadcasted_iota(jnp.int32, (td, td), 0)
    qry = lax.broadcasted_iota(jnp.int32, (td, td), 1)
    diag_valid = _chunk_of(key) <= _chunk_of(qry)

    def q_tile(qi, c):
        r0 = pl.multiple_of(qi * tq, tq)
        q = q_ref[pl.ds(r0, tq), :]

        def kv(j, carry):
            k0 = pl.multiple_of(j * tk, tk)
            return _online_softmax_step(q, k_ref[pl.ds(k0, tk), :], v_ref[pl.ds(k0, tk), :],
                                        carry, None)

        m, l, acc = lax.fori_loop(
            0, qi * (tq // tk), kv,
            (jnp.full((1, tq), -jnp.inf, F32), jnp.zeros((1, tq), F32), jnp.zeros((V_DIM, tq), F32)))
        for e in range(tq // td):
            rows = slice(e * td, (e + 1) * td)
            part = (m[:, rows], l[:, rows], acc[:, rows])
            qe = q[rows]
            if e > 0:
                k0 = pl.multiple_of(r0, td)
                part = _online_softmax_step(qe, k_ref[pl.ds(k0, e * td), :],
                                            v_ref[pl.ds(k0, e * td), :], part, None)
            k0 = pl.multiple_of(r0 + e * td, td)
            _, le, acce = _online_softmax_step(qe, k_ref[pl.ds(k0, td), :], v_ref[pl.ds(k0, td), :],
                                               part, diag_valid)
            o_ref[pl.ds(pl.multiple_of(r0 + e * td, td), td), :] = (acce / le).T.astype(o_ref.dtype)
        return c

    lax.fori_loop(0, nq, q_tile, 0)


def _latent_attention(q, k, v, b, s, tq=2048, tk=1024, td=1024):
    tq = min(tq, s)
    tk, td = min(tk, tq), min(td, tq)
    assert s % tq == 0 and tq % tk == 0 and tq % td == 0 and td % CHUNK == 0
    qk_spec = pl.BlockSpec((None, s, PADDED_HEAD), lambda bi, h: (bi, 0, h))
    v_spec = pl.BlockSpec((None, s, V_DIM), lambda bi, h: (bi, 0, h))
    out = pl.pallas_call(
        functools.partial(_mla_kernel, nq=s // tq, tq=tq, tk=tk, td=td),
        out_shape=jax.ShapeDtypeStruct((b, s, H_C * V_DIM), BF16),
        grid=(b, H_C),
        in_specs=[qk_spec, qk_spec, v_spec],
        out_specs=v_spec,
        compiler_params=_params("arbitrary", "arbitrary"),
        name="latent_attn",
    )(q.reshape(b, s, H_C * PADDED_HEAD), k.reshape(b, s, H_C * PADDED_HEAD),
      v.reshape(b, s, H_C * V_DIM))
    return out.reshape(b * s, H_C * V_DIM)


def _mla_sample_kernel(q_ref, k_ref, v_ref, o_ref, *, n_valid, qpos0):
    t, tk = q_ref.shape[0], k_ref.shape[0]
    row = lax.broadcasted_iota(jnp.int32, (t, tk), 0)
    col = lax.broadcasted_iota(jnp.int32, (t, tk), 1)
    valid = jnp.logical_and(col < n_valid, _chunk_of(col) <= _chunk_of(qpos0 + row))
    bias = jnp.where(valid, 0.0, -jnp.inf).astype(F32)
    for g in range(q_ref.shape[1] // PADDED_HEAD):
        qk = slice(g * PADDED_HEAD, (g + 1) * PADDED_HEAD)
        vo = slice(g * V_DIM, (g + 1) * V_DIM)
        o_ref[:, vo] = _softmax_attend(q_ref[:, qk], k_ref[:, qk], v_ref[:, vo],
                                       bias).astype(o_ref.dtype)


def _latent_attention_sample(q, k, v, bs, t, tk, n_valid, qpos0, heads_per_step=4):
    g = heads_per_step
    return pl.pallas_call(
        functools.partial(_mla_sample_kernel, n_valid=n_valid, qpos0=qpos0),
        out_shape=jax.ShapeDtypeStruct((bs * t, H_C * V_DIM), BF16),
        grid=(bs, H_C // g),
        in_specs=[pl.BlockSpec((t, g * PADDED_HEAD), lambda bi, h: (bi, h)),
                  pl.BlockSpec((None, tk, g * PADDED_HEAD), lambda bi, h: (bi, 0, h)),
                  pl.BlockSpec((None, tk, g * V_DIM), lambda bi, h: (bi, 0, h))],
        out_specs=pl.BlockSpec((t, g * V_DIM), lambda bi, h: (bi, h)),
        compiler_params=_params("arbitrary", "arbitrary"),
        name="latent_attn_sample",
    )(q, k, v)


def _rope_tables(pos):
    half = ROPE_DIM // 2
    inv = ROPE_THETA ** (-jnp.arange(half, dtype=F32) / half)
    ang = pos.astype(F32)[:, None] * inv[None, :]
    cos, sin = jnp.cos(ang), jnp.sin(ang)
    return (jnp.concatenate([cos, cos, cos, cos], -1), jnp.concatenate([-sin, sin, -sin, sin], -1))


def _swap_halves(w):
    half = ROPE_DIM // 2
    return jnp.concatenate([w[..., half:], w[..., :half]], -1)


def _rope_gains(g):
    z = jnp.zeros((LANES - g.shape[0],), g.dtype)
    return jnp.concatenate([g, z])[None, :], jnp.concatenate([z, g])[None, :]


class _Stream:
    def __init__(self, x, row0, pos0):
        self.b, self.s, self.d = x.shape
        self.m = self.b * self.s
        self.row0 = row0
        self.pos0 = pos0
        self.per_row_gate = self.s < 128


def _gate_operand(st, mod, ls):
    d = st.d
    if st.per_row_gate:
        g = jnp.repeat(mod[ls, st.row0:st.row0 + st.b, 2 * d:], st.s, axis=0)
        return g, lambda tm, tn: pl.BlockSpec((tm, tn), lambda i, j, kk: (i, j))
    mod4 = mod.reshape(mod.shape[0], mod.shape[1], 1, 3 * d)

    def spec(tm, tn):
        per_b = st.s // tm
        return pl.BlockSpec((None, None, 1, tn),
                            lambda i, j, kk: (ls, st.row0 + i // per_b, 0, 2 * d // tn + j))
    return mod4, spec


def _mlp_sublayer(st, x2d, mod, norm_g4, ls, w1, w2, layer):
    d = st.d
    mod4 = mod.reshape(mod.shape[0], mod.shape[1], 1, 3 * d)
    h = _adanorm(x2d.reshape(st.b, st.s, d), norm_g4, mod4, ls, st.row0)
    dff = w1.shape[-1]
    tm = _row_tile(st.m)
    tn = min(512, dff)
    res = _matmul(h, w1, 0, dff, tn, _epi_relu2, [],
                  [((st.m, dff), BF16, (tm, tn), lambda i, j: (i, j))], "mlp_up", layer=layer)
    if len(res) == 2:
        w1 = res[1]
    gate, gate_spec = _gate_operand(st, mod, ls)
    out = _resid_matmul(res[0], w2, x2d, gate, gate_spec, "mlp_down", layer=layer)
    if isinstance(out, tuple):
        out, w2 = out
    return out, (w1, w2)


def _even_layer(st, x2d, mod, norm_g4, ls, w_in, g_q, g_k, bias, w_out, cache):
    d = st.d
    m = st.m
    mod4 = mod.reshape(mod.shape[0], mod.shape[1], 1, 3 * d)
    h = _adanorm(x2d.reshape(st.b, st.s, d), norm_g4, mod4, ls, st.row0)
    tm = _row_tile(m)
    wa = H_A * HEAD_DIM
    wb = H_B * HEAD_DIM
    d_ab = wa + wb
    ns = 2 if tm >= 512 else 1
    tn = 512 * ns
    scale = HEAD_DIM ** -0.5 * LOG2E
    blk = lambda i, j: (i, j)
    bf = lambda: ((m, wa), BF16, (tm, tn), blk)
    f32 = lambda: ((m, wa), F32, (tm, tn), blk)

    def gain(g):
        return [(g[None, :], (1, HEAD_DIM), lambda i, j: (0, 0))]

    reuse = {}

    def proj(col0, epi, extras, outs, name):
        w, c0 = (w_in[name], 0) if isinstance(w_in, dict) else (w_in, col0)
        res = _matmul(h, w, c0, wa, tn, epi, extras, outs, name, n_split=ns)
        if len(res) > len(outs):
            reuse[name] = res[-1]
        return res[:len(outs)]

    qa, = proj(0, functools.partial(_epi_headnorm, scale=scale), gain(g_q), [bf()], "proj_qa")
    qb, = proj(wa, functools.partial(_epi_plain, scale=scale), [], [bf()], "proj_qb")
    ka, ka32 = proj(d_ab, _epi_headnorm, gain(g_k), [bf(), f32()], "proj_ka")
    kb, kb32 = proj(d_ab + wa, _epi_plain, [], [bf(), f32()], "proj_kb")
    va, va32 = proj(2 * d_ab, _epi_plain, [], [bf(), f32()], "proj_va")
    vb, vb32 = proj(2 * d_ab + wa, _epi_plain, [], [bf(), f32()], "proj_vb")

    if cache is None:
        oa = _band_attention(qa, ka, va, bias, st.b, st.s)
        t = min(MXU_DIM, st.s)
        v3 = lambda a: a.reshape(st.b, st.s, wb)
        ob = _stick_breaking(v3(qb), v3(kb), v3(vb), st.b, st.s, st.s, t, t, 0).reshape(m, wb)
    else:
        ca_k, ca_v, cb_k, cb_v = cache
        n_keep = ca_k.shape[1]
        assert st.s <= CHUNK and n_keep + st.s <= BAND_W and n_keep == BAND_PAST

        def with_cache(c, new, total):
            c = c.reshape(st.b, c.shape[1], -1).astype(BF16)
            new = new.reshape(st.b, st.s, -1)
            pad = jnp.zeros((st.b, total - c.shape[1] - st.s, c.shape[2]), BF16)
            return jnp.concatenate([c, new, pad], 1)

        oa = _band_attention_sample(qa, with_cache(ca_k, ka, BAND_W), with_cache(ca_v, va, BAND_W),
                                    bias, st.b, st.s, n_keep + st.s)
        past = cb_k.shape[1]
        tk = MXU_DIM
        assert past % tk == 0
        total = past + tk
        ob = _stick_breaking(qb.reshape(st.b, st.s, wb), with_cache(cb_k, kb, total),
                             with_cache(cb_v, vb, total), st.b, st.s, total, st.s, tk, past)
        ob = ob.reshape(m, wb)

    gate, gate_spec = _gate_operand(st, mod, ls)
    x2d = _resid_matmul([oa, ob], w_out, x2d, gate, gate_spec, "out_ab")
    if isinstance(x2d, tuple):
        x2d, reuse["out"] = x2d
    r4 = lambda a, hh: a.reshape(st.b, st.s, hh, HEAD_DIM)
    return x2d, (r4(ka32, H_A), r4(va32, H_A), r4(kb32, H_B), r4(vb32, H_B)), reuse


def _odd_layer(st, x2d, mod, norm_g4, ls, wq_a, g_q_lat, wq_b, wkv_c, wkv_pe, g_kv_lat, wkv_bk, wkv_bv,
               g_q, g_k, w_out, cache):
    d = st.d
    m = st.m
    mod4 = mod.reshape(mod.shape[0], mod.shape[1], 1, 3 * d)
    h = _adanorm(x2d.reshape(st.b, st.s, d), norm_g4, mod4, ls, st.row0)
    tm = _row_tile(m)
    blk = lambda i, j: (i, j)
    row_blk = lambda i, j: (i, 0)
    one = lambda i, j: (0, 0)
    q_lora = wq_a.shape[1]
    kv_lora = wkv_c.shape[1]

    cos, sin = _rope_tables(st.pos0 + jnp.arange(st.s))
    per_b = st.s // tm if st.s >= tm else 0
    if per_b:
        tab_blk = lambda i, j: (i % per_b, 0)
    else:
        cos, sin = jnp.tile(cos, (st.b, 1)), jnp.tile(sin, (st.b, 1))
        tab_blk = row_blk
    tables = [(cos, (tm, LANES), tab_blk), (sin, (tm, LANES), tab_blk)]

    q_lat, = _matmul(h, wq_a, 0, q_lora, q_lora, _epi_rownorm,
                     [(g_q_lat[None, :], (1, q_lora), one)],
                     [((m, q_lora), BF16, (tm, q_lora), blk)], "proj_q_lat")
    c_kv, c_kv32 = _matmul(h, wkv_c, 0, kv_lora, kv_lora, _epi_rownorm,
                           [(g_kv_lat[None, :], (1, kv_lora), one)],
                           [((m, kv_lora), BF16, (tm, kv_lora), blk),
                            ((m, kv_lora), F32, (tm, kv_lora), blk)], "proj_c_kv")
    k_pe, = _matmul(h, wkv_pe, 0, 2 * LANES, 2 * LANES, _epi_rope_shared, tables,
                    [((m, 2 * LANES), F32, (tm, 2 * LANES), blk)], "proj_k_pe")

    wide = H_C * PADDED_HEAD
    gq_n, gq_r = g_q[None, :NOPE_DIM], _rope_gains(g_q[NOPE_DIM:])
    gk_n, gk_r = g_k[None, :NOPE_DIM], _rope_gains(g_k[NOPE_DIM:])
    lane_vec = lambda g: (g, (1, LANES), one)
    ns = 2 if tm >= 512 else 1
    q, = _matmul(q_lat, wq_b, 0, wide, 512 * ns,
                 functools.partial(_epi_q_latent, scale=QK_DIM ** -0.5 * LOG2E),
                 tables + [lane_vec(gq_n), lane_vec(gq_r[0]), lane_vec(gq_r[1])],
                 [((m, wide), BF16, (tm, 512 * ns), blk)], "proj_q", n_split=ns)

    if cache is None:
        c_all, pe_all, mk = c_kv, k_pe, m
    else:
        cc_kv, cc_pe = cache
        past = cc_kv.shape[1]
        total = past + MXU_DIM
        pad = total - past - st.s
        c_all = jnp.concatenate([cc_kv.astype(BF16), c_kv.reshape(st.b, st.s, kv_lora),
                                 jnp.zeros((st.b, pad, kv_lora), BF16)], 1)
        gap = jnp.zeros(cc_pe.shape[:2] + (2 * (LANES - ROPE_DIM),), F32)
        pe_c = jnp.concatenate([cc_pe, gap, cc_pe], -1)
        pe_all = jnp.concatenate([pe_c, k_pe.reshape(st.b, st.s, 2 * LANES),
                                  jnp.zeros((st.b, pad, 2 * LANES), F32)], 1)
        mk = st.b * total
        c_all, pe_all = c_all.reshape(mk, kv_lora), pe_all.reshape(mk, 2 * LANES)

    tmk = _row_tile(mk)
    nk = H_C * NOPE_DIM
    nsk = 2 if tmk >= 512 else 1
    k, = _matmul(c_all, wkv_bk, 0, nk, 512 * nsk, _epi_k_latent,
                 [(pe_all, (tmk, 2 * LANES), row_blk), lane_vec(gk_n), lane_vec(gk_r[0]),
                  lane_vec(gk_r[1])],
                 [((mk, wide), BF16, (tmk, 2 * 512 * nsk), blk)], "proj_k", n_split=nsk)
    v, = _matmul(c_all, wkv_bv, 0, H_C * V_DIM, 512, _epi_plain, [],
                 [((mk, H_C * V_DIM), BF16, (tmk, 512), blk)], "proj_v")

    if cache is None:
        o = _latent_attention(q, k, v, st.b, st.s)
    else:
        o = _latent_attention_sample(q, k.reshape(st.b, total, wide),
                                     v.reshape(st.b, total, H_C * V_DIM),
                                     st.b, st.s, total, past + st.s, st.pos0)
    gate, gate_spec = _gate_operand(st, mod, ls)
    x2d = _resid_matmul(o, w_out, x2d, gate, gate_spec, "out_c")
    if isinstance(x2d, tuple):
        x2d, w_out = x2d
    keep = (c_kv32.reshape(st.b, st.s, kv_lora), k_pe[:, :ROPE_DIM].reshape(st.b, st.s, ROPE_DIM))
    return x2d, keep, w_out


def kernel(x_prompt, x_sample, cache_a_k, cache_a_v, cache_b_k, cache_b_v, cache_c_kv, cache_c_pe,
           c_prompt, c_sample, norm_g, ada_w, ada_b, w_in_ab, g_q_a, g_k_a, rel_bias_a, w_out_ab,
           wq_a_c, g_q_lat_c, wq_b_c, wkv_a_c, g_kv_lat_c, wkv_b_c, g_q_c, g_k_c, w_out_c,
           mlp_w1, mlp_w2):
    bp, s, d = x_prompt.shape
    bs, t, _ = x_sample.shape
    depth = norm_g.shape[0]
    past_len = cache_b_k.shape[2]
    n_keep = cache_a_k.shape[2]
    assert s >= n_keep

    n_rows = -(-(bp + bs) // 16) * 16
    c_rows = jnp.concatenate([c_prompt, c_sample, jnp.zeros((n_rows - bp - bs, d), F32)], 0)
    mod = _ada_params(c_rows, ada_w.reshape(depth * 2, d, 3 * d), ada_b.reshape(depth * 2, 1, 3 * d))
    norm_g4 = norm_g.reshape(depth * 2, 1, d)
    w2_all = mlp_w2.astype(BF16)

    prompt = _Stream(x_prompt, 0, 0)
    sample = _Stream(x_sample, bp, past_len)
    xp = x_prompt.reshape(bp * s, d)
    xs = x_sample.reshape(bs * t, d)

    outs_p = {"a": [], "c": []}
    outs_s = {"a": [], "c": []}
    for layer in range(depth):
        ls = 2 * layer
        if layer % 2 == 0:
            e = layer // 2
            bias = _band_bias(rel_bias_a[e])
            w_out = w_out_ab[e].astype(BF16)
            xs, keep, w_bf16 = _even_layer(
                sample, xs, mod, norm_g4, ls, w_in_ab[e], g_q_a[e], g_k_a[e], bias, w_out,
                (cache_a_k[e], cache_a_v[e], cache_b_k[e], cache_b_v[e]))
            outs_s["a"].append(keep)
            xp, keep, _ = _even_layer(prompt, xp, mod, norm_g4, ls, w_bf16, g_q_a[e], g_k_a[e], bias,
                                      w_out, None)
            outs_p["a"].append(keep)
        else:
            mi = layer // 2
            kv_lora = g_kv_lat_c.shape[1]
            wq_a = wq_a_c[mi].astype(BF16)
            wq_b = wq_b_c[mi].astype(BF16).reshape(-1, H_C // 2, 2, QK_DIM)
            q_lora = wq_b.shape[0]
            wq_n = wq_b[..., :NOPE_DIM].reshape(q_lora, H_C // 2, 2 * NOPE_DIM)
            wq_r = wq_b[..., NOPE_DIM:]
            wq_b = jnp.concatenate([wq_n, wq_r.reshape(q_lora, H_C // 2, 2 * ROPE_DIM),
                                    _swap_halves(wq_r).reshape(q_lora, H_C // 2, 2 * ROPE_DIM)], -1)
            wq_b = wq_b.reshape(q_lora, H_C * PADDED_HEAD)
            wkv_a = wkv_a_c[mi]
            wkv_c = wkv_a[:, :kv_lora].astype(BF16)
            wkv_pe = wkv_a[:, kv_lora:].astype(BF16)
            wkv_sw = _swap_halves(wkv_pe)
            wkv_pe = jnp.concatenate([wkv_pe, wkv_sw, wkv_sw, wkv_pe], -1)
            wkv_b = wkv_b_c[mi].reshape(kv_lora, H_C, NOPE_DIM + V_DIM)
            wkv_bk = wkv_b[..., :NOPE_DIM].reshape(kv_lora, H_C * NOPE_DIM).astype(BF16)
            wkv_bv = wkv_b[..., NOPE_DIM:].reshape(kv_lora, H_C * V_DIM).astype(BF16)
            args = (wq_a, g_q_lat_c[mi], wq_b, wkv_c, wkv_pe, g_kv_lat_c[mi], wkv_bk, wkv_bv,
                    g_q_c[mi], g_k_c[mi])
            xs, keep, w_out = _odd_layer(sample, xs, mod, norm_g4, ls, *args, w_out_c[mi].astype(BF16),
                                         (cache_c_kv[mi], cache_c_pe[mi]))
            outs_s["c"].append(keep)
            xp, keep, _ = _odd_layer(prompt, xp, mod, norm_g4, ls, *args, w_out, None)
            outs_p["c"].append(keep)
        xs, (w1, w2) = _mlp_sublayer(sample, xs, mod, norm_g4, ls + 1, mlp_w1, w2_all, layer)
        xp, _ = _mlp_sublayer(prompt, xp, mod, norm_g4, ls + 1, w1, w2, layer)

    stack = lambda items, i: jnp.stack([it[i] for it in items])
    tail = lambda a: a[:, :, s - n_keep:]
    return (xp.reshape(bp, s, d), xs.reshape(bs, t, d),
            tail(stack(outs_p["a"], 0)), tail(stack(outs_p["a"], 1)),
            stack(outs_s["a"], 0), stack(outs_s["a"], 1),
            stack(outs_p["a"], 2), stack(outs_p["a"], 3),
            stack(outs_s["a"], 2), stack(outs_s["a"], 3),
            stack(outs_p["c"], 0), stack(outs_p["c"], 1),
            stack(outs_s["c"], 0), stack(outs_s["c"], 1))
```
